```python
import math
import jax, jax.numpy as jnp
from jax import lax
import numpy as np

D_MODEL = 4096
BATCH = 2
SEQ = 4096
DEPTH = 2

N_META = 16
MIX_WIDTH = D_MODEL
W_HYENA = MIX_WIDTH // 4
W_FNET = MIX_WIDTH // 4
W_CONV = MIX_WIDTH // 4
W_ATTN = MIX_WIDTH - W_HYENA - W_FNET - W_CONV
HEAD_DIM = 128
N_Q_HEADS = W_ATTN // HEAD_DIM
N_KV_HEADS = 2
Q_PER_KV = N_Q_HEADS // N_KV_HEADS
KV_WIDTH = N_KV_HEADS * HEAD_DIM
WINDOW = 128
BLOCK = 128
FNET_GROUPS = 4
FNET_GROUP_W = W_FNET // FNET_GROUPS
SHORT_CONV = 3
DW_CONV = 31
FILTER_EMB = 33
FILTER_HIDDEN = 64
DECAY_MIN = -math.log(1e-2) / 1.5
DECAY_MAX = -math.log(1e-2) / 0.3
D_FF = 11008
N_EXPERTS = 8
TOP_K = 2
D_EXPERT = 4096
N_DENSE = (DEPTH + 1) // 2
N_MOE = DEPTH // 2
EPS = 1e-6
NEG = -1e30
IN_SIZES = [3 * W_HYENA, W_FNET, 2 * W_CONV, W_ATTN, KV_WIDTH, KV_WIDTH]
P_IN = sum(IN_SIZES)

kernel_name = 'hymba_style_hybrid_encoder'

F32 = jnp.float32


def _rmsnorm(x, g):
    xf = x.astype(F32)
    y = xf * lax.rsqrt(jnp.mean(xf * xf, axis=-1, keepdims=True) + EPS)
    return (y * g.astype(F32)).astype(x.dtype)


def _layernorm(x, g, b):
    xf = x.astype(F32)
    mu = jnp.mean(xf, axis=-1, keepdims=True)
    xc = xf - mu
    var = jnp.mean(xc * xc, axis=-1, keepdims=True)
    return (xc * lax.rsqrt(var + EPS) * g.astype(F32) + b.astype(F32)).astype(x.dtype)


def _depthwise_conv(u, w, b):
    K, C = w.shape
    y = lax.conv_general_dilated(u, w[:, None, :].astype(u.dtype), window_strides=(1,),
                                 padding=[(K // 2, K // 2)],
                                 dimension_numbers=('NWC', 'WIO', 'NWC'),
                                 feature_group_count=C)
    return y + b.astype(u.dtype)


def _hyena_filters(L, w1, b1, w2, b2, w3, b3, wo, freq, decay):
    t = jnp.linspace(0.0, 1.0, L, dtype=F32)[:, None]
    bands = (FILTER_EMB - 1) // 2
    w = 2.0 * math.pi * jnp.arange(L, dtype=F32)[:, None] / L
    f = jnp.linspace(1e-4, bands - 1, bands, dtype=F32)[None, :]
    z = jnp.concatenate([t, jnp.cos(f * w), -jnp.sin(f * w)], axis=-1)
    fr = freq.astype(F32)
    hid = jnp.sin(fr * (z @ w1.astype(F32) + b1.astype(F32)))
    hid = jnp.sin(fr * (hid @ w2.astype(F32) + b2.astype(F32)))
    hid = jnp.sin(fr * (hid @ w3.astype(F32) + b3.astype(F32)))
    h = (hid @ wo.astype(F32)).reshape(L, 2, W_HYENA)
    h = h * jnp.exp(-t[:, :, None] * decay.astype(F32)[None])
    return h[:, 0], h[:, 1]


def _bidir_long_conv(u, h_fwd, h_bwd):
    L, C = h_fwd.shape
    k = jnp.concatenate([h_fwd[:1] + h_bwd[:1], h_fwd[1:], jnp.zeros((1, C), F32),
                         h_bwd[1:][::-1]], axis=0)
    U = jnp.fft.rfft(u.astype(F32), n=2 * L, axis=1)
    Kf = jnp.fft.rfft(k, axis=0)
    y = jnp.fft.irfft(U * Kf[None], n=2 * L, axis=1)[:, :L]
    return y.astype(u.dtype)


def _hyena(u, conv_w, conv_b, h_fwd, h_bwd, skip):
    u = _depthwise_conv(u, conv_w, conv_b)
    x0, x1, v = jnp.split(u, 3, axis=-1)
    v = v * x1
    v = _bidir_long_conv(v, h_fwd, h_bwd) + skip.astype(v.dtype) * v
    return v * x0


def _fnet(u):
    B, L, C = u.shape
    ug = u.astype(F32).reshape(B, L, FNET_GROUPS, FNET_GROUP_W)
    y = jnp.fft.fft2(ug, axes=(1, 3), norm='ortho').real
    return y.reshape(B, L, C).astype(u.dtype)


def _conformer_conv(u, dw_w, dw_b, ln_g, ln_b):
    a, g = jnp.split(u, 2, axis=-1)
    y = a * jax.nn.sigmoid(g)
    y = _depthwise_conv(y, dw_w, dw_b)
    y = _layernorm(y, ln_g, ln_b)
    return jax.nn.silu(y)


def _alibi_slopes():
    s = np.array([2.0 ** (-8.0 * (h + 1) / N_Q_HEADS) for h in range(N_Q_HEADS)], dtype=np.float32)
    return jnp.asarray(s).reshape(N_KV_HEADS, Q_PER_KV)


def _dist_valid(qpos, kpos, kmeta, kin):
    delta = jnp.abs(qpos[:, :, None] - kpos[:, None, :])
    dist = jnp.where(kmeta[:, None, :], 0, delta).astype(F32)
    valid = kmeta[:, None, :] | (kin[:, None, :] & (delta <= WINDOW))
    return dist, valid


def _band_attend(q, k, v, dist, valid, sink):
    slopes = _alibi_slopes()
    s = jnp.einsum('bnqkgd,bnskd->bnkgqs', q, k).astype(F32) * (1.0 / math.sqrt(HEAD_DIM))
    s = s - slopes[:, :, None, None] * dist[:, None, None]
    s = jnp.where(valid[:, None, None], s, NEG)
    sk = sink.astype(F32).reshape(N_KV_HEADS, Q_PER_KV)[:, :, None, None]
    m = jnp.maximum(jnp.max(s, axis=-1, keepdims=True), sk)
    e = jnp.exp(s - m)
    p = e / (jnp.sum(e, axis=-1, keepdims=True) + jnp.exp(sk - m))
    return jnp.einsum('bnkgqs,bnskd->bnqkgd', p.astype(v.dtype), v)


def _windowed_gqa(q, k, v, sink):
    B, L = q.shape[0], q.shape[1]
    n = L - N_META
    nb = n // BLOCK
    qm, qr = q[:, :N_META], q[:, N_META:]
    km, kr = k[:, :N_META], k[:, N_META:]
    vm, vr = v[:, :N_META], v[:, N_META:]

    def band(t):
        tp = jnp.pad(t, ((0, 0), (BLOCK, BLOCK), (0, 0), (0, 0))).reshape(B, nb + 2, BLOCK, N_KV_HEADS, HEAD_DIM)
        return jnp.concatenate([tp[:, :-2], tp[:, 1:-1], tp[:, 2:]], axis=2)

    def with_meta(tm, tr):
        tmb = jnp.broadcast_to(tm[:, None], (B, nb, N_META, N_KV_HEADS, HEAD_DIM))
        return jnp.concatenate([tmb, band(tr)], axis=2)

    qb = qr.reshape(B, nb, BLOCK, N_KV_HEADS, Q_PER_KV, HEAD_DIM)
    kb, vb = with_meta(km, kr), with_meta(vm, vr)
    blk = jnp.arange(nb)[:, None]
    qpos = N_META + blk * BLOCK + jnp.arange(BLOCK)[None]
    kreal = (blk - 1) * BLOCK + jnp.arange(3 * BLOCK)[None]
    kpos = jnp.concatenate([jnp.broadcast_to(jnp.arange(N_META)[None], (nb, N_META)), N_META + kreal], axis=1)
    kmeta = jnp.concatenate([jnp.ones((nb, N_META), bool), jnp.zeros((nb, 3 * BLOCK), bool)], axis=1)
    kin = jnp.concatenate([jnp.ones((nb, N_META), bool), (kreal >= 0) & (kreal < n)], axis=1)
    dist, valid = _dist_valid(qpos, kpos, kmeta, kin)
    o_r = _band_attend(qb, kb, vb, dist, valid, sink).reshape(B, n, W_ATTN)

    qmb = qm.reshape(B, 1, N_META, N_KV_HEADS, Q_PER_KV, HEAD_DIM)
    kmb = jnp.concatenate([km, kr[:, :BLOCK]], axis=1)[:, None]
    vmb = jnp.concatenate([vm, vr[:, :BLOCK]], axis=1)[:, None]
    qpos_m = jnp.arange(N_META)[None]
    kpos_m = jnp.concatenate([jnp.arange(N_META), N_META + jnp.arange(BLOCK)])[None]
    kmeta_m = jnp.concatenate([jnp.ones((N_META,), bool), jnp.zeros((BLOCK,), bool)])[None]
    kin_m = jnp.ones((1, N_META + BLOCK), bool)
    dist_m, valid_m = _dist_valid(qpos_m, kpos_m, kmeta_m, kin_m)
    o_m = _band_attend(qmb, kmb, vmb, dist_m, valid_m, sink).reshape(B, N_META, W_ATTN)
    return jnp.concatenate([o_m, o_r], axis=1)


def _swiglu(x, w1, w3, w2):
    return (jax.nn.silu(x @ w1) * (x @ w3)) @ w2


def _moe(x, router_w, router_b, w1, w3, w2):
    logits = (x @ router_w).astype(F32) + router_b.astype(F32)
    top_v, top_i = lax.top_k(logits, TOP_K)
    top_w = jax.nn.softmax(top_v, axis=-1)
    gates = jnp.sum(jax.nn.one_hot(top_i, N_EXPERTS, dtype=F32) * top_w[..., None], axis=-2)
    y = jnp.zeros_like(x)
    for e in range(N_EXPERTS):
        y = y + gates[..., e:e + 1].astype(x.dtype) * _swiglu(x, w1[e], w3[e], w2[e])
    return y


def setup_inputs(seed: int = 0) -> dict:
    key = jax.random.key(seed)
    ks = iter(jax.random.split(key, 40))

    def nrm(shape, scale):
        return jax.random.normal(next(ks), shape, F32) * scale

    D = D_MODEL
    return {
        'x': nrm((BATCH, SEQ, D), 1.0),
        'meta_tokens': nrm((N_META, D), 1.0),
        'norm_mix_g': 1.0 + nrm((DEPTH, D), 0.02),
        'w_in': nrm((DEPTH, D, P_IN), D ** -0.5),
        'hy_conv_w': nrm((DEPTH, SHORT_CONV, 3 * W_HYENA), SHORT_CONV ** -0.5),
        'hy_conv_b': nrm((DEPTH, 3 * W_HYENA), 0.02),
        'hy_f_w1': nrm((DEPTH, FILTER_EMB, FILTER_HIDDEN), FILTER_EMB ** -0.5),
        'hy_f_b1': nrm((DEPTH, FILTER_HIDDEN), 0.02),
        'hy_f_w2': nrm((DEPTH, FILTER_HIDDEN, FILTER_HIDDEN), FILTER_HIDDEN ** -0.5),
        'hy_f_b2': nrm((DEPTH, FILTER_HIDDEN), 0.02),
        'hy_f_w3': nrm((DEPTH, FILTER_HIDDEN, FILTER_HIDDEN), FILTER_HIDDEN ** -0.5),
        'hy_f_b3': nrm((DEPTH, FILTER_HIDDEN), 0.02),
        'hy_f_wo': nrm((DEPTH, FILTER_HIDDEN, 2 * W_HYENA), FILTER_HIDDEN ** -0.5),
        'hy_f_freq': 1.0 + nrm((DEPTH, FILTER_HIDDEN), 0.02),
        'hy_decay': jax.random.uniform(next(ks), (DEPTH, 2, W_HYENA), F32, DECAY_MIN, DECAY_MAX),
        'hy_skip': nrm((DEPTH, W_HYENA), 0.5),
        'cv_dw_w': nrm((DEPTH, DW_CONV, W_CONV), DW_CONV ** -0.5),
        'cv_dw_b': nrm((DEPTH, W_CONV), 0.02),
        'cv_ln_g': 1.0 + nrm((DEPTH, W_CONV), 0.02),
        'cv_ln_b': nrm((DEPTH, W_CONV), 0.02),
        'attn_sink': nrm((DEPTH, N_Q_HEADS), 1.0),
        'group_norm_g': 1.0 + nrm((DEPTH, MIX_WIDTH), 0.02),
        'w_out': nrm((DEPTH, MIX_WIDTH, D), MIX_WIDTH ** -0.5),
        'norm_ffn_g': 1.0 + nrm((DEPTH, D), 0.02),
        'ffn_w1': nrm((N_DENSE, D, D_FF), D ** -0.5),
        'ffn_w3': nrm((N_DENSE, D, D_FF), D ** -0.5),
        'ffn_w2': nrm((N_DENSE, D_FF, D), D_FF ** -0.5),
        'router_w': nrm((N_MOE, D, N_EXPERTS), D ** -0.5),
        'router_b': nrm((N_MOE, N_EXPERTS), 0.01),
        'moe_w1': nrm((N_MOE, N_EXPERTS, D, D_EXPERT), D ** -0.5),
        'moe_w3': nrm((N_MOE, N_EXPERTS, D, D_EXPERT), D ** -0.5),
        'moe_w2': nrm((N_MOE, N_EXPERTS, D_EXPERT, D), D_EXPERT ** -0.5),
        'final_norm_g': 1.0 + nrm((D,), 0.02),
    }


def reference(x, meta_tokens, norm_mix_g, w_in, hy_conv_w, hy_conv_b, hy_f_w1, hy_f_b1, hy_f_w2,
              hy_f_b2, hy_f_w3, hy_f_b3, hy_f_wo, hy_f_freq, hy_decay, hy_skip, cv_dw_w, cv_dw_b,
              cv_ln_g, cv_ln_b, attn_sink, group_norm_g, w_out, norm_ffn_g, ffn_w1, ffn_w3, ffn_w2,
              router_w, router_b, moe_w1, moe_w3, moe_w2, final_norm_g):
    B = x.shape[0]
    h = jnp.concatenate([jnp.broadcast_to(meta_tokens[None].astype(x.dtype), (B, N_META, D_MODEL)), x], axis=1)
    L = h.shape[1]
    in_offsets = list(np.cumsum(IN_SIZES)[:-1])
    g_offsets = [W_HYENA, W_HYENA + W_FNET, W_HYENA + W_FNET + W_CONV]
    for l in range(DEPTH):
        xn = _rmsnorm(h, norm_mix_g[l])
        proj = xn @ w_in[l]
        u_a, u_b, u_c, q, k, v = jnp.split(proj, in_offsets, axis=-1)
        h_fwd, h_bwd = _hyena_filters(L, hy_f_w1[l], hy_f_b1[l], hy_f_w2[l], hy_f_b2[l], hy_f_w3[l],
                                      hy_f_b3[l], hy_f_wo[l], hy_f_freq[l], hy_decay[l])
        y_a = _hyena(u_a, hy_conv_w[l], hy_conv_b[l], h_fwd, h_bwd, hy_skip[l])
        y_b = _fnet(u_b)
        y_c = _conformer_conv(u_c, cv_dw_w[l], cv_dw_b[l], cv_ln_g[l], cv_ln_b[l])
        y_d = _windowed_gqa(q.reshape(B, L, N_Q_HEADS, HEAD_DIM), k.reshape(B, L, N_KV_HEADS, HEAD_DIM),
                            v.reshape(B, L, N_KV_HEADS, HEAD_DIM), attn_sink[l])
        g_a, g_b, g_c, g_d = jnp.split(group_norm_g[l], g_offsets)
        y = jnp.concatenate([_rmsnorm(y_a, g_a), _rmsnorm(y_b, g_b), _rmsnorm(y_c, g_c), _rmsnorm(y_d, g_d)], axis=-1)
        h = h + y @ w_out[l]
        xn = _rmsnorm(h, norm_ffn_g[l])
        if l % 2 == 0:
            i = l // 2
            h = h + _swiglu(xn, ffn_w1[i], ffn_w3[i], ffn_w2[i])
        else:
            i = l // 2
            h = h + _moe(xn, router_w[i], router_b[i], moe_w1[i], moe_w3[i], moe_w2[i])
    return _rmsnorm(h, final_norm_g)[:, N_META:]
```

```python
import functools
import math

import numpy as np
import jax
import jax.numpy as jnp
from jax import lax
from jax.experimental import pallas as pl
from jax.experimental.pallas import tpu as pltpu

F32 = jnp.float32
BF16 = jnp.bfloat16

V7X_LANES = 128
V7X_BF16_ROWS = 16
V7X_VMEM_LIMIT = 58 * 1024 * 1024

EPS = 1e-6
NEG = -1e30
HEAD_DIM = 128
N_KV_HEADS = 2
ATT_BLOCK = 128
FNET_GROUP_W = 256
N_EXPERTS = 8
MOE_TILE = 256


def _round_up(x, m):
    return (x + m - 1) // m * m


def _pick_tile(n, candidates):
    for c in candidates:
        if n % c == 0:
            return c
    raise ValueError(f"no tile for {n} in {candidates}")


def _params(sem, vmem=None):
    if sem is None:
        return pltpu.CompilerParams(vmem_limit_bytes=vmem or V7X_VMEM_LIMIT)
    return pltpu.CompilerParams(dimension_semantics=sem, vmem_limit_bytes=vmem or V7X_VMEM_LIMIT)


def _rmsnorm_kernel(x_ref, g_ref, o_ref):
    x = x_ref[...]
    ms = jnp.mean(x * x, axis=-1, keepdims=True)
    o_ref[...] = (x * lax.rsqrt(ms + EPS) * g_ref[...]).astype(o_ref.dtype)


def _rmsnorm(x2d, g, out_dtype, tm=256):
    m, d = x2d.shape
    return pl.pallas_call(
        _rmsnorm_kernel,
        grid=(pl.cdiv(m, tm),),
        in_specs=[pl.BlockSpec((tm, d), lambda i: (i, 0)), pl.BlockSpec((1, d), lambda i: (0, 0))],
        out_specs=pl.BlockSpec((tm, d), lambda i: (i, 0)),
        out_shape=jax.ShapeDtypeStruct((m, d), out_dtype),
        compiler_params=_params(("arbitrary",)),
    )(x2d, g.reshape(1, d))


def _rmsnorm_real_kernel(x_ref, g_ref, o_ref):
    x = x_ref[...]
    ms = jnp.mean(x * x, axis=-1, keepdims=True)
    o_ref[...] = x * lax.rsqrt(ms + EPS) * g_ref[...]


def _rmsnorm_real(h3, g, n_real, tm=256):
    b, _, d = h3.shape
    return pl.pallas_call(
        _rmsnorm_real_kernel,
        grid=(b, n_real // tm),
        in_specs=[pl.BlockSpec((None, tm, d), lambda bi, i: (bi, i, 0)),
                  pl.BlockSpec((1, d), lambda bi, i: (0, 0))],
        out_specs=pl.BlockSpec((None, tm, d), lambda bi, i: (bi, i, 0)),
        out_shape=jax.ShapeDtypeStruct((b, n_real, d), F32),
        compiler_params=_params(("arbitrary", "arbitrary")),
    )(h3, g.reshape(1, d))


def _cast_tile(w_ref, wb_ref):
    k = w_ref.shape[0]
    ck = _pick_tile(k, (256, 128, 64, 32, 16))

    def body(c, carry):
        r = pl.multiple_of(c * ck, ck)
        wb_ref[pl.ds(r, ck), :] = w_ref[pl.ds(r, ck), :].astype(BF16)
        return carry

    lax.fori_loop(0, k // ck, body, 0)


def _mm_kernel(*refs, n_w, swiglu, has_res):
    x_ref = refs[0]
    w_refs = refs[1:1 + n_w]
    pos = 1 + n_w
    res_ref = refs[pos] if has_res else None
    pos += int(has_res)
    o_ref = refs[pos]
    wb_refs = refs[pos + 1:pos + 1 + n_w]

    @pl.when(pl.program_id(1) == 0)
    def _():
        for w_ref, wb_ref in zip(w_refs, wb_refs):
            _cast_tile(w_ref, wb_ref)

    x = x_ref[...]
    acc = jnp.dot(x, wb_refs[0][...], preferred_element_type=F32)
    if swiglu:
        up = jnp.dot(x, wb_refs[1][...], preferred_element_type=F32)
        acc = acc * jax.nn.sigmoid(acc) * up
    if has_res:
        acc = acc + res_ref[...]
    o_ref[...] = acc.astype(o_ref.dtype)


def _matmul(x, ws, *, tk, k_blk, n_out, tn, tm, out_dtype, residual=None, swiglu=False):
    m = x.shape[0]
    n_w = len(ws)
    tm = min(tm, _round_up(m, V7X_BF16_ROWS))
    in_specs = [pl.BlockSpec((tm, tk), lambda j, i: (i, k_blk))]
    in_specs += [pl.BlockSpec((tk, tn), lambda j, i: (k_blk, j)) for _ in ws]
    args = [x, *ws]
    if residual is not None:
        in_specs.append(pl.BlockSpec((tm, tn), lambda j, i: (i, j)))
        args.append(residual)
    return pl.pallas_call(
        functools.partial(_mm_kernel, n_w=n_w, swiglu=swiglu, has_res=residual is not None),
        grid=(n_out // tn, pl.cdiv(m, tm)),
        in_specs=in_specs,
        out_specs=pl.BlockSpec((tm, tn), lambda j, i: (i, j)),
        out_shape=jax.ShapeDtypeStruct((m, n_out), out_dtype),
        scratch_shapes=[pltpu.VMEM((tk, tn), BF16) for _ in ws],
        compiler_params=_params(("arbitrary", "arbitrary")),
    )(*args)


def _gmm_kernel(te_ref, tv_ref, *refs, n_w, swiglu):
    x_ref = refs[0]
    w_refs = refs[1:1 + n_w]
    o_ref = refs[1 + n_w]
    wb_refs = refs[2 + n_w:2 + 2 * n_w]
    t = pl.program_id(1)
    prev = te_ref[jnp.maximum(t - 1, 0)]

    @pl.when((t == 0) | (te_ref[t] != prev))
    def _():
        for w_ref, wb_ref in zip(w_refs, wb_refs):
            _cast_tile(w_ref, wb_ref)

    @pl.when(tv_ref[t] != 0)
    def _():
        x = x_ref[...]
        acc = jnp.dot(x, wb_refs[0][...], preferred_element_type=F32)
        if swiglu:
            up = jnp.dot(x, wb_refs[1][...], preferred_element_type=F32)
            acc = acc * jax.nn.sigmoid(acc) * up
        o_ref[...] = acc.astype(o_ref.dtype)

    @pl.when(tv_ref[t] == 0)
    def _():
        o_ref[...] = jnp.zeros_like(o_ref)


def _grouped_matmul(x, ws, tile_expert, tile_valid, *, tn, out_dtype, swiglu=False):
    s, k = x.shape
    n = ws[0].shape[2]
    n_w = len(ws)
    grid_spec = pltpu.PrefetchScalarGridSpec(
        num_scalar_prefetch=2,
        grid=(n // tn, s // MOE_TILE),
        in_specs=[pl.BlockSpec((MOE_TILE, k), lambda j, t, te, tv: (t, 0))]
        + [pl.BlockSpec((None, k, tn), lambda j, t, te, tv: (te[t], 0, j)) for _ in ws],
        out_specs=pl.BlockSpec((MOE_TILE, tn), lambda j, t, te, tv: (t, j)),
        scratch_shapes=[pltpu.VMEM((k, tn), BF16) for _ in ws],
    )
    return pl.pallas_call(
        functools.partial(_gmm_kernel, n_w=n_w, swiglu=swiglu),
        grid_spec=grid_spec,
        out_shape=jax.ShapeDtypeStruct((s, n), out_dtype),
        compiler_params=_params(("arbitrary", "arbitrary")),
    )(tile_expert, tile_valid, x, *ws)


def _dft_table_kernel(sc_ref, ss_ref, rc_ref, rs_ref, c_ref, s_ref, *, n_blocks, shift, n_rows_out):
    rc = rc_ref[...]
    rs = rs_ref[...]

    def body(q, carry):
        c, s = carry
        ob = q + shift
        ob = jnp.where(ob >= n_blocks, ob - n_blocks, ob)
        row = pl.multiple_of(ob * 16, 16)
        c_ref[pl.ds(row, 16), :] = c.astype(BF16)
        s_ref[pl.ds(row, 16), :] = s.astype(BF16)
        return c * rc - s * rs, s * rc + c * rs

    lax.fori_loop(0, n_blocks, body, (sc_ref[...], ss_ref[...]))
    pad = n_rows_out - 16 * n_blocks
    if pad:
        c_ref[pl.ds(16 * n_blocks, pad), :] = jnp.zeros((pad, c_ref.shape[1]), BF16)
        s_ref[pl.ds(16 * n_blocks, pad), :] = jnp.zeros((pad, s_ref.shape[1]), BF16)


def _dft_tables(col_pos, col_valid, unit, period, n_blocks, shift, n_rows_out):
    ncol = col_pos.shape[0]
    f = np.arange(16, dtype=np.int64)[:, None]
    p = col_pos.astype(np.int64)[None, :]
    ang = unit * ((f * p) % period)
    valid = col_valid[None, :].astype(np.float64)
    seed_c = (np.cos(ang) * valid).astype(np.float32)
    seed_s = (np.sin(ang) * valid).astype(np.float32)
    ang16 = unit * ((16 * p) % period)
    rot_c = np.cos(ang16).astype(np.float32)
    rot_s = np.sin(ang16).astype(np.float32)
    pc = _pick_tile(ncol, (384, 256, 128))
    kern = functools.partial(_dft_table_kernel, n_blocks=n_blocks, shift=shift, n_rows_out=n_rows_out)
    return pl.pallas_call(
        kern,
        grid=(ncol // pc,),
        in_specs=[pl.BlockSpec((16, pc), lambda j: (0, j)), pl.BlockSpec((16, pc), lambda j: (0, j)),
                  pl.BlockSpec((1, pc), lambda j: (0, j)), pl.BlockSpec((1, pc), lambda j: (0, j))],
        out_specs=[pl.BlockSpec((n_rows_out, pc), lambda j: (0, j)),
                   pl.BlockSpec((n_rows_out, pc), lambda j: (0, j))],
        out_shape=[jax.ShapeDtypeStruct((n_rows_out, ncol), BF16)] * 2,
        compiler_params=_params(("arbitrary",)),
    )(jnp.asarray(seed_c), jnp.asarray(seed_s), jnp.asarray(rot_c), jnp.asarray(rot_s))


def _filter_kernel(z_ref, w1_ref, b1_ref, w2_ref, b2_ref, w3_ref, b3_ref, wo_ref, fr_ref, dec_ref,
                   hs_ref, hd_ref, *, width):
    hp = lax.Precision.HIGHEST
    z = z_ref[...]
    fr = fr_ref[...]
    hid = jnp.sin(fr * (jnp.dot(z, w1_ref[...], precision=hp, preferred_element_type=F32) + b1_ref[...]))
    hid = jnp.sin(fr * (jnp.dot(hid, w2_ref[...], precision=hp, preferred_element_type=F32) + b2_ref[...]))
    hid = jnp.sin(fr * (jnp.dot(hid, w3_ref[...], precision=hp, preferred_element_type=F32) + b3_ref[...]))
    h = jnp.dot(hid, wo_ref[...], precision=hp, preferred_element_type=F32)
    t = z[:, 0:1]
    valid = z[:, V7X_LANES - 1:V7X_LANES]
    h = h * jnp.exp(-t * dec_ref[...]) * valid
    hf = h[:, :width]
    hb = h[:, width:]
    hs_ref[...] = (hf + hb).astype(BF16)
    hd_ref[...] = (hf - hb).astype(BF16)


def _hyena_filter_sums(zc, w1, b1, w2, b2, w3, b3, wo, freq, decay, width):
    lp = zc.shape[0]
    hid = w2.shape[0]
    tr = _pick_tile(lp, (528, 384, 256, 128))
    w1p = jnp.zeros((V7X_LANES, hid), F32).at[:w1.shape[0]].set(w1)
    full = lambda shape: pl.BlockSpec(shape, lambda i: (0, 0))
    return pl.pallas_call(
        functools.partial(_filter_kernel, width=width),
        grid=(lp // tr,),
        in_specs=[pl.BlockSpec((tr, V7X_LANES), lambda i: (i, 0)),
                  full((V7X_LANES, hid)), full((1, hid)), full((hid, hid)), full((1, hid)),
                  full((hid, hid)), full((1, hid)), full((hid, 2 * width)), full((1, hid)),
                  full((1, 2 * width))],
        out_specs=[pl.BlockSpec((tr, width), lambda i: (i, 0))] * 2,
        out_shape=[jax.ShapeDtypeStruct((lp, width), BF16)] * 2,
        compiler_params=_params(("arbitrary",)),
    )(zc, w1p, b1.reshape(1, hid), w2, b2.reshape(1, hid), w3, b3.reshape(1, hid), wo,
      freq.reshape(1, hid), decay.reshape(1, 2 * width))


def _kf_kernel(c_ref, s_ref, hs_ref, hd_ref, wf_ref, kre_ref, kim_ref):
    wf = wf_ref[...]
    kre_ref[...] = wf * jnp.dot(c_ref[...], hs_ref[...], preferred_element_type=F32)
    kim_ref[...] = -wf * jnp.dot(s_ref[...], hd_ref[...], preferred_element_type=F32)


def _hyena_filter_spectrum(cos_t, sin_t, hs, hd, wf):
    fp, lp = cos_t.shape
    width = hs.shape[1]
    tf = _pick_tile(fp, (384, 256, 128))
    cb = 256
    return pl.pallas_call(
        _kf_kernel,
        grid=(width // cb, fp // tf),
        in_specs=[pl.BlockSpec((tf, lp), lambda c, f: (f, 0)), pl.BlockSpec((tf, lp), lambda c, f: (f, 0)),
                  pl.BlockSpec((lp, cb), lambda c, f: (0, c)), pl.BlockSpec((lp, cb), lambda c, f: (0, c)),
                  pl.BlockSpec((tf, 1), lambda c, f: (f, 0))],
        out_specs=[pl.BlockSpec((tf, cb), lambda c, f: (f, c))] * 2,
        out_shape=[jax.ShapeDtypeStruct((fp, width), F32)] * 2,
        compiler_params=_params(("arbitrary", "arbitrary")),
    )(cos_t, sin_t, hs, hd, wf)


def _hyena_prep_kernel(x0_ref, x1_ref, v_ref, w0_ref, w1_ref, wv_ref, b0_ref, b1_ref, bv_ref,
                       x0c_ref, vx_ref, pad_ref, *, n_real, n_meta, lp):
    cb = pad_ref.shape[1]
    l = n_real + n_meta
    pad_ref[pl.ds(0, 8), :] = jnp.zeros((8, cb), F32)
    pad_ref[pl.ds(8 + l, 8), :] = jnp.zeros((8, cb), F32)

    def conv(u_ref, w_ref, b_ref):
        pad_ref[pl.ds(8, n_meta), :] = u_ref[pl.ds(n_real, n_meta), :]
        pad_ref[pl.ds(8 + n_meta, n_real), :] = u_ref[pl.ds(0, n_real), :]
        w = w_ref[...]
        b = b_ref[...]

        def at(start, n):
            return (w[0:1] * pad_ref[pl.ds(start - 1, n), :] + w[1:2] * pad_ref[pl.ds(start, n), :]
                    + w[2:3] * pad_ref[pl.ds(start + 1, n), :] + b)

        return at(8 + n_meta, n_real), at(8, n_meta)

    x0_real, x0_meta = conv(x0_ref, w0_ref, b0_ref)
    x0c_ref[pl.ds(0, n_real), :] = x0_real
    x0c_ref[pl.ds(n_real, n_meta), :] = x0_meta
    x1_real, x1_meta = conv(x1_ref, w1_ref, b1_ref)
    v_real, v_meta = conv(v_ref, wv_ref, bv_ref)
    vx_ref[pl.ds(0, n_real), :] = (v_real * x1_real).astype(BF16)
    vx_ref[pl.ds(n_real, n_meta), :] = (v_meta * x1_meta).astype(BF16)
    vx_ref[pl.ds(l, lp - l), :] = jnp.zeros((lp - l, cb), BF16)


def _hyena_prep(proj3, conv_w, conv_b, width, n_real, n_meta, lp):
    b, l, _ = proj3.shape
    cb = 128
    nc = width // cb
    col = lambda off: pl.BlockSpec((None, l, cb), lambda bi, c: (bi, 0, off + c))
    wcol = lambda off: pl.BlockSpec((3, cb), lambda bi, c: (0, off + c))
    bcol = lambda off: pl.BlockSpec((1, cb), lambda bi, c: (0, off + c))
    cbias = conv_b.reshape(1, 3 * width)
    return pl.pallas_call(
        functools.partial(_hyena_prep_kernel, n_real=n_real, n_meta=n_meta, lp=lp),
        grid=(b, nc),
        in_specs=[col(0), col(nc), col(2 * nc), wcol(0), wcol(nc), wcol(2 * nc),
                  bcol(0), bcol(nc), bcol(2 * nc)],
        out_specs=[pl.BlockSpec((None, l, cb), lambda bi, c: (bi, 0, c)),
                   pl.BlockSpec((None, lp, cb), lambda bi, c: (bi, 0, c))],
        out_shape=[jax.ShapeDtypeStruct((b, l, width), F32), jax.ShapeDtypeStruct((b, lp, width), BF16)],
        scratch_shapes=[pltpu.VMEM((l + 16, cb), F32)],
        compiler_params=_params(("arbitrary", "arbitrary")),
    )(proj3, proj3, proj3, conv_w, conv_w, conv_w, cbias, cbias, cbias)


def _hyena_fwd_kernel(c_ref, s_ref, vx_ref, kre_ref, kim_ref, yre_ref, yim_ref):
    vx = vx_ref[...]
    ure = jnp.dot(c_ref[...], vx, preferred_element_type=F32)
    uim = jnp.dot(s_ref[...], vx, preferred_element_type=F32)
    kre = kre_ref[...]
    kim = kim_ref[...]
    yre_ref[...] = (ure * kre + uim * kim).astype(BF16)
    yim_ref[...] = (ure * kim - uim * kre).astype(BF16)


def _hyena_fwd(cos_t, sin_t, vx, kre, kim):
    fp, lp = cos_t.shape
    b, _, width = vx.shape
    tf = _pick_tile(fp, (384, 256, 128))
    cb = 256
    return pl.pallas_call(
        _hyena_fwd_kernel,
        grid=(width // cb, b, fp // tf),
        in_specs=[pl.BlockSpec((tf, lp), lambda c, bi, f: (f, 0)),
                  pl.BlockSpec((tf, lp), lambda c, bi, f: (f, 0)),
                  pl.BlockSpec((None, lp, cb), lambda c, bi, f: (bi, 0, c)),
                  pl.BlockSpec((tf, cb), lambda c, bi, f: (f, c)),
                  pl.BlockSpec((tf, cb), lambda c, bi, f: (f, c))],
        out_specs=[pl.BlockSpec((None, tf, cb), lambda c, bi, f: (bi, f, c))] * 2,
        out_shape=[jax.ShapeDtypeStruct((b, fp, width), BF16)] * 2,
        compiler_params=_params(("arbitrary", "arbitrary", "arbitrary")),
    )(cos_t, sin_t, vx, kre, kim)


def _hyena_inv_kernel(ct_ref, st_ref, yre_ref, yim_ref, vx_ref, x0c_ref, skip_ref, o_ref):
    y = (jnp.dot(ct_ref[...], yre_ref[...], preferred_element_type=F32)
         - jnp.dot(st_ref[...], yim_ref[...], preferred_element_type=F32))
    o_ref[...] = (y + skip_ref[...] * vx_ref[...].astype(F32)) * x0c_ref[...]


def _hyena_inv(cos_tt, sin_tt, yre, yim, vx, x0c, skip):
    lp, fp = cos_tt.shape
    b, l, width = x0c.shape
    tr = _pick_tile(lp, (384, 256, 128))
    cb = 256
    return pl.pallas_call(
        _hyena_inv_kernel,
        grid=(width // cb, b, lp // tr),
        in_specs=[pl.BlockSpec((tr, fp), lambda c, bi, r: (r, 0)),
                  pl.BlockSpec((tr, fp), lambda c, bi, r: (r, 0)),
                  pl.BlockSpec((None, fp, cb), lambda c, bi, r: (bi, 0, c)),
                  pl.BlockSpec((None, fp, cb), lambda c, bi, r: (bi, 0, c)),
                  pl.BlockSpec((None, tr, cb), lambda c, bi, r: (bi, r, c)),
                  pl.BlockSpec((None, tr, cb), lambda c, bi, r: (bi, r, c)),
                  pl.BlockSpec((1, cb), lambda c, bi, r: (0, c))],
        out_specs=pl.BlockSpec((None, tr, cb), lambda c, bi, r: (bi, r, c)),
        out_shape=jax.ShapeDtypeStruct((b, l, width), F32),
        compiler_params=_params(("arbitrary", "arbitrary", "arbitrary")),
    )(cos_tt, sin_tt, yre, yim, vx, x0c, skip.reshape(1, width))


def _fnet_chan_kernel(u_ref, cc_ref, sc_ref, a_ref, b_ref, *, l, tr):
    row = pl.program_id(1) * tr + lax.broadcasted_iota(jnp.int32, (tr, 1), 0)
    u = jnp.where(row < l, u_ref[...], 0.0).astype(BF16)
    gw = FNET_GROUP_W
    for g in range(u.shape[1] // gw):
        ug = u[:, g * gw:(g + 1) * gw]
        a_ref[:, g * gw:(g + 1) * gw] = jnp.dot(ug, cc_ref[...], preferred_element_type=F32).astype(BF16)
        b_ref[:, g * gw:(g + 1) * gw] = jnp.dot(ug, sc_ref[...], preferred_element_type=F32).astype(BF16)


def _fnet_chan(proj3, col_blk, width, lp):
    b, l, _ = proj3.shape
    tr = _pick_tile(lp, (384, 256, 128))
    k = np.arange(FNET_GROUP_W)
    ang = 2.0 * np.pi * ((k[:, None] * k[None, :]) % FNET_GROUP_W) / FNET_GROUP_W
    cc = jnp.asarray(np.cos(ang), BF16)
    sc = jnp.asarray(np.sin(ang), BF16)
    return pl.pallas_call(
        functools.partial(_fnet_chan_kernel, l=l, tr=tr),
        grid=(b, lp // tr),
        in_specs=[pl.BlockSpec((None, tr, width), lambda bi, r: (bi, r, col_blk)),
                  pl.BlockSpec((FNET_GROUP_W, FNET_GROUP_W), lambda bi, r: (0, 0)),
                  pl.BlockSpec((FNET_GROUP_W, FNET_GROUP_W), lambda bi, r: (0, 0))],
        out_specs=[pl.BlockSpec((None, tr, width), lambda bi, r: (bi, r, 0))] * 2,
        out_shape=[jax.ShapeDtypeStruct((b, lp, width), BF16)] * 2,
        compiler_params=_params(("arbitrary", "arbitrary")),
    )(proj3, cc, sc)


def _fnet_time_kernel(c_ref, s_ref, a_ref, b_ref, o_ref, *, scale):
    y = (jnp.dot(c_ref[...], a_ref[...], preferred_element_type=F32)
         - jnp.dot(s_ref[...], b_ref[...], preferred_element_type=F32))
    o_ref[...] = y * scale


def _fnet_time(cos_l, sin_l, a, bm, l):
    lp = cos_l.shape[0]
    b, _, width = a.shape
    tr = _pick_tile(lp, (384, 256, 128))
    cb = 512
    scale = 1.0 / math.sqrt(l * FNET_GROUP_W)
    return pl.pallas_call(
        functools.partial(_fnet_time_kernel, scale=scale),
        grid=(width // cb, b, lp // tr),
        in_specs=[pl.BlockSpec((tr, lp), lambda c, bi, r: (r, 0)),
                  pl.BlockSpec((tr, lp), lambda c, bi, r: (r, 0)),
                  pl.BlockSpec((None, lp, cb), lambda c, bi, r: (bi, 0, c)),
                  pl.BlockSpec((None, lp, cb), lambda c, bi, r: (bi, 0, c))],
        out_specs=pl.BlockSpec((None, tr, cb), lambda c, bi, r: (bi, r, c)),
        out_shape=jax.ShapeDtypeStruct((b, l, width), F32),
        compiler_params=_params(("arbitrary", "arbitrary", "arbitrary")),
    )(cos_l, sin_l, a, bm)


def _conformer_kernel(a_ref, g_ref, w_ref, b_ref, o_ref, pad_ref, *, n_real, n_meta, taps, chunk):
    cb = pad_ref.shape[1]
    half = taps // 2
    lead = _round_up(half, 8)
    l = n_real + n_meta
    pad_ref[pl.ds(0, lead), :] = jnp.zeros((lead, cb), F32)
    pad_ref[pl.ds(lead + l, lead), :] = jnp.zeros((lead, cb), F32)
    pad_ref[pl.ds(lead, n_meta), :] = (a_ref[pl.ds(n_real, n_meta), :]
                                       * jax.nn.sigmoid(g_ref[pl.ds(n_real, n_meta), :]))
    pad_ref[pl.ds(lead + n_meta, n_real), :] = (a_ref[pl.ds(0, n_real), :]
                                                * jax.nn.sigmoid(g_ref[pl.ds(0, n_real), :]))
    w = w_ref[...]
    bias = b_ref[...]

    def window(start, n):
        acc = jnp.broadcast_to(bias, (n, cb))
        for j in range(taps):
            acc = acc + w[j:j + 1] * pad_ref[pl.ds(start + j - half, n), :]
        return acc

    def body(ci, carry):
        base = pl.multiple_of(ci * chunk, chunk)
        o_ref[pl.ds(base, chunk), :] = window(base + lead + n_meta, chunk)
        return carry

    lax.fori_loop(0, n_real // chunk, body, 0)
    o_ref[pl.ds(n_real, n_meta), :] = window(lead, n_meta)


def _conformer_conv(proj3, col_blk, dw_w, dw_b, width, n_real, n_meta):
    b, l, _ = proj3.shape
    taps = dw_w.shape[0]
    cb = 128
    nc = width // cb
    lead = _round_up(taps // 2, 8)
    return pl.pallas_call(
        functools.partial(_conformer_kernel, n_real=n_real, n_meta=n_meta, taps=taps, chunk=64),
        grid=(b, nc),
        in_specs=[pl.BlockSpec((None, l, cb), lambda bi, c: (bi, 0, col_blk + c)),
                  pl.BlockSpec((None, l, cb), lambda bi, c: (bi, 0, col_blk + nc + c)),
                  pl.BlockSpec((taps, cb), lambda bi, c: (0, c)),
                  pl.BlockSpec((1, cb), lambda bi, c: (0, c))],
        out_specs=pl.BlockSpec((None, l, cb), lambda bi, c: (bi, 0, c)),
        out_shape=jax.ShapeDtypeStruct((b, l, width), F32),
        scratch_shapes=[pltpu.VMEM((l + 2 * lead, cb), F32)],
        compiler_params=_params(("arbitrary", "arbitrary")),
    )(proj3, proj3, dw_w, dw_b.reshape(1, width))


def _softmax_pv(parts, sink, out_dtype=F32):
    m = sink
    for s, _ in parts:
        m = jnp.maximum(m, jnp.max(s, axis=-1, keepdims=True))
    denom = jnp.exp(sink - m)
    acc = None
    for s, v in parts:
        e = jnp.exp(s - m)
        denom = denom + jnp.sum(e, axis=-1, keepdims=True)
        pv = jnp.dot(e.astype(BF16), v, preferred_element_type=F32)
        acc = pv if acc is None else acc + pv
    return (acc / denom).astype(out_dtype)


def _qk(q, k):
    return lax.dot_general(q, k, (((1,), (1,)), ((), ())), preferred_element_type=F32)


def _attn_kernel(slope_ref, sink_ref, q_ref, kp_ref, kc_ref, kn_ref, km_ref, vp_ref, vc_ref, vn_ref, vm_ref,
                 o_ref, *, nb, n_meta, q_per_kv):
    g = pl.program_id(1)
    i = pl.program_id(2)
    t = ATT_BLOCK
    a = lax.broadcasted_iota(jnp.int32, (t, t), 0)
    c = lax.broadcasted_iota(jnp.int32, (t, t), 1)
    scale = 1.0 / math.sqrt(HEAD_DIM)
    km = km_ref[...].astype(BF16)
    vm = vm_ref[...].astype(BF16)
    kn = kn_ref[...].astype(BF16)
    vn = vn_ref[...].astype(BF16)

    def heads(fn):
        for hh in range(q_per_kv):
            head = g * q_per_kv + hh
            q = (q_ref[:, hh * HEAD_DIM:(hh + 1) * HEAD_DIM] * scale).astype(BF16)
            o_ref[:, hh * HEAD_DIM:(hh + 1) * HEAD_DIM] = fn(q, slope_ref[head], sink_ref[head])

    @pl.when(i < nb)
    def _():
        d_prev = (t + a - c).astype(F32)
        d_cur = jnp.abs(a - c).astype(F32)
        d_next = (t + c - a).astype(F32)
        ok_prev = c >= a + jnp.where(i > 0, 0, t)
        ok_next = c <= a - jnp.where(i < nb - 1, 0, t)
        kp, kc = kp_ref[...].astype(BF16), kc_ref[...].astype(BF16)
        vp, vc = vp_ref[...].astype(BF16), vc_ref[...].astype(BF16)

        def real(q, slope, sink):
            s_m = _qk(q, km)
            s_p = jnp.where(ok_prev, _qk(q, kp) - slope * d_prev, NEG)
            s_c = _qk(q, kc) - slope * d_cur
            s_n = jnp.where(ok_next, _qk(q, kn) - slope * d_next, NEG)
            return _softmax_pv([(s_m, vm), (s_p, vp), (s_c, vc), (s_n, vn)], sink)

        heads(real)

    @pl.when(i == nb)
    def _():
        delta = n_meta + c - a
        ok = delta <= t
        dist = delta.astype(F32)

        def meta(q, slope, sink):
            s_m = _qk(q, km)
            s_0 = jnp.where(ok, _qk(q, kn) - slope * dist, NEG)
            return _softmax_pv([(s_m, vm), (s_0, vn)], sink)

        heads(meta)


def _windowed_attention(proj3, sink, q_col, k_col, v_col, n_q_heads, n_real, n_meta):
    b, l, _ = proj3.shape
    nb = n_real // ATT_BLOCK
    q_per_kv = n_q_heads // N_KV_HEADS
    qw = q_per_kv * HEAD_DIM
    qb, kb, vb = q_col // qw, k_col // HEAD_DIM, v_col // HEAD_DIM
    mb = n_real // n_meta
    slopes = jnp.asarray([2.0 ** (-8.0 * (h + 1) / n_q_heads) for h in range(n_q_heads)], F32)
    smem = pl.BlockSpec(memory_space=pltpu.SMEM)

    def kv_blk(colb, shift):
        def index(bi, g, i):
            blk = jnp.clip(i + shift, 0, nb - 1)
            if shift == 1:
                blk = jnp.where(i == nb, 0, blk)
            return bi, blk, colb + g
        return pl.BlockSpec((None, ATT_BLOCK, HEAD_DIM), index)

    kv_meta = lambda colb: pl.BlockSpec((None, n_meta, HEAD_DIM), lambda bi, g, i: (bi, mb, colb + g))
    return pl.pallas_call(
        functools.partial(_attn_kernel, nb=nb, n_meta=n_meta, q_per_kv=q_per_kv),
        grid=(b, N_KV_HEADS, nb + 1),
        in_specs=[smem, smem,
                  pl.BlockSpec((None, ATT_BLOCK, qw), lambda bi, g, i: (bi, i, qb + g)),
                  kv_blk(kb, -1), kv_blk(kb, 0), kv_blk(kb, 1), kv_meta(kb),
                  kv_blk(vb, -1), kv_blk(vb, 0), kv_blk(vb, 1), kv_meta(vb)],
        out_specs=pl.BlockSpec((None, ATT_BLOCK, qw), lambda bi, g, i: (bi, i, g)),
        out_shape=jax.ShapeDtypeStruct((b, l, n_q_heads * HEAD_DIM), F32),
        compiler_params=_params(("arbitrary", "arbitrary", "arbitrary")),
    )(slopes, sink, proj3, proj3, proj3, proj3, proj3, proj3, proj3, proj3, proj3)


def _group_norm_kernel(ya_ref, yb_ref, yc_ref, yd_ref, g_ref, lng_ref, lnb_ref, o_ref, *, widths):
    def rms(y, g):
        return y * lax.rsqrt(jnp.mean(y * y, axis=-1, keepdims=True) + EPS) * g

    off = 0
    for idx, (y_ref, w) in enumerate(zip((ya_ref, yb_ref, yc_ref, yd_ref), widths)):
        y = y_ref[...]
        if idx == 2:
            mu = jnp.mean(y, axis=-1, keepdims=True)
            yc = y - mu
            var = jnp.mean(yc * yc, axis=-1, keepdims=True)
            y = yc * lax.rsqrt(var + EPS) * lng_ref[...] + lnb_ref[...]
            y = y * jax.nn.sigmoid(y)
        o_ref[:, off:off + w] = rms(y, g_ref[:, off:off + w]).astype(o_ref.dtype)
        off += w


def _group_norms(ya, yb, yc, yd, group_g, ln_g, ln_b, tm=256):
    m = ya.shape[0]
    widths = (ya.shape[1], yb.shape[1], yc.shape[1], yd.shape[1])
    tot = sum(widths)
    row = lambda w: pl.BlockSpec((tm, w), lambda i: (i, 0))
    full = lambda w: pl.BlockSpec((1, w), lambda i: (0, 0))
    return pl.pallas_call(
        functools.partial(_group_norm_kernel, widths=widths),
        grid=(pl.cdiv(m, tm),),
        in_specs=[row(widths[0]), row(widths[1]), row(widths[2]), row(widths[3]),
                  full(tot), full(widths[2]), full(widths[2])],
        out_specs=row(tot),
        out_shape=jax.ShapeDtypeStruct((m, tot), BF16),
        compiler_params=_params(("arbitrary",)),
    )(ya, yb, yc, yd, group_g.reshape(1, tot), ln_g.reshape(1, -1), ln_b.reshape(1, -1))


def _router_kernel(x_ref, w_ref, o_ref):
    x = x_ref[...].astype(BF16)
    o_ref[...] = lax.dot_general(w_ref[...], x, (((1,), (1,)), ((), ())), preferred_element_type=F32)


def _router_logits(xn, router_w, tm=256):
    m, d = xn.shape
    wt = router_w.T.astype(BF16)
    return pl.pallas_call(
        _router_kernel,
        grid=(m // tm,),
        in_specs=[pl.BlockSpec((tm, d), lambda i: (i, 0)), pl.BlockSpec((N_EXPERTS, d), lambda i: (0, 0))],
        out_specs=pl.BlockSpec((N_EXPERTS, tm), lambda i: (0, i)),
        out_shape=jax.ShapeDtypeStruct((N_EXPERTS, m), F32),
        compiler_params=_params(("arbitrary",)),
    )(xn, wt)


def _route_kernel(lg_ref, b_ref, slot1_ref, slot2_ref, g1_ref, g2_ref, te_ref, tv_ref, *, chunk, n_tiles):
    ne, m = lg_ref.shape
    logits = lg_ref[...] + b_ref[...]
    row = lax.broadcasted_iota(jnp.int32, (ne, m), 0)
    m1 = jnp.max(logits, axis=0, keepdims=True)
    e1 = jnp.min(jnp.where(logits == m1, row, ne), axis=0, keepdims=True)
    rest = jnp.where(row == e1, -jnp.inf, logits)
    m2 = jnp.max(rest, axis=0, keepdims=True)
    e2 = jnp.min(jnp.where(rest == m2, row, ne), axis=0, keepdims=True)
    g1 = 1.0 / (1.0 + jnp.exp(m2 - m1))
    g1_ref[...] = g1
    g2_ref[...] = 1.0 - g1
    oh1 = (row == e1).astype(F32)
    oh2 = (row == e2).astype(F32)
    c1 = jnp.sum(oh1, axis=1, keepdims=True)
    c2 = jnp.sum(oh2, axis=1, keepdims=True)
    padded = jnp.floor((c1 + c2 + (MOE_TILE - 1)) / MOE_TILE) * MOE_TILE
    erow = lax.broadcasted_iota(jnp.int32, (ne, 1), 0)
    start = jnp.zeros((ne, 1), F32)
    for e in range(ne - 1):
        start = start + jnp.where(erow > e, padded[e:e + 1, :], 0.0)
    end = start + padded
    tri = (lax.broadcasted_iota(jnp.int32, (chunk, chunk), 0)
           < lax.broadcasted_iota(jnp.int32, (chunk, chunk), 1)).astype(BF16)
    carry = jnp.concatenate([jnp.zeros((ne, 1), F32), c1], axis=0)
    base1 = start
    for ci in range(m // chunk):
        sl = slice(ci * chunk, (ci + 1) * chunk)
        oh = jnp.concatenate([oh1[:, sl], oh2[:, sl]], axis=0)
        pre = jnp.dot(oh.astype(BF16), tri, preferred_element_type=F32) + carry
        carry = carry + jnp.sum(oh, axis=1, keepdims=True)
        s1 = jnp.sum(oh[:ne] * (pre[:ne] + base1), axis=0, keepdims=True)
        s2 = jnp.sum(oh[ne:] * (pre[ne:] + base1), axis=0, keepdims=True)
        slot1_ref[:, sl] = s1.astype(jnp.int32)
        slot2_ref[:, sl] = s2.astype(jnp.int32)
    tile0 = (lax.broadcasted_iota(jnp.int32, (1, V7X_LANES), 1) * MOE_TILE).astype(F32)
    te = jnp.sum((end <= tile0).astype(jnp.int32), axis=0, keepdims=True)
    te_ref[...] = jnp.minimum(te, ne - 1)
    tv_ref[...] = (tile0 < end[ne - 1:ne, :]).astype(jnp.int32)
    del n_tiles


def _route(logits_t, router_b, n_tiles):
    ne, m = logits_t.shape
    assert n_tiles <= V7X_LANES
    vec = lambda dt: jax.ShapeDtypeStruct((1, m), dt)
    lane = jax.ShapeDtypeStruct((1, V7X_LANES), jnp.int32)
    return pl.pallas_call(
        functools.partial(_route_kernel, chunk=_pick_tile(m, (512, 256, 128)), n_tiles=n_tiles),
        out_shape=[vec(jnp.int32), vec(jnp.int32), vec(F32), vec(F32), lane, lane],
        compiler_params=_params(None),
    )(logits_t, router_b.reshape(ne, 1))


def _invert_kernel(slot1_ref, slot2_ref, tok_ref, *, n_slots, m):
    def zero(s, carry):
        tok_ref[s] = 0
        return carry

    lax.fori_loop(0, n_slots, zero, 0)

    def put(t, carry):
        tok_ref[slot1_ref[t]] = t
        tok_ref[slot2_ref[t]] = t
        return carry

    lax.fori_loop(0, m, put, 0)


def _invert_slots(slot1, slot2, n_slots):
    m = slot1.shape[0]
    smem = pl.BlockSpec(memory_space=pltpu.SMEM)
    return pl.pallas_call(
        functools.partial(_invert_kernel, n_slots=n_slots, m=m),
        in_specs=[smem, smem],
        out_specs=smem,
        out_shape=jax.ShapeDtypeStruct((n_slots,), jnp.int32),
        compiler_params=_params(None),
    )(slot1, slot2)


def _row_copy(src_hbm, dst_vmem, sem, src_row, dst_row):
    return pltpu.make_async_copy(src_hbm.at[pl.ds(src_row, 1)], dst_vmem.at[pl.ds(dst_row, 1)], sem)


def _gather_kernel(tok_ref, x_hbm, o_ref, buf_ref, sem):
    n = buf_ref.shape[0]

    def start(r, carry):
        _row_copy(x_hbm, buf_ref, sem, tok_ref[0, 0, r], r).start()
        return carry

    lax.fori_loop(0, n, start, 0)

    def wait(r, carry):
        _row_copy(x_hbm, buf_ref, sem, 0, r).wait()
        return carry

    lax.fori_loop(0, n, wait, 0)
    o_ref[...] = buf_ref[...].astype(o_ref.dtype)


def _gather_rows(xn, tok):
    _, d = xn.shape
    n_slots = tok.shape[0]
    n_tiles = n_slots // MOE_TILE
    return pl.pallas_call(
        _gather_kernel,
        grid=(n_tiles,),
        in_specs=[pl.BlockSpec((1, 1, MOE_TILE), lambda t: (t, 0, 0), memory_space=pltpu.SMEM),
                  pl.BlockSpec(memory_space=pl.ANY)],
        out_specs=pl.BlockSpec((MOE_TILE, d), lambda t: (t, 0)),
        out_shape=jax.ShapeDtypeStruct((n_slots, d), BF16),
        scratch_shapes=[pltpu.VMEM((MOE_TILE, d), F32), pltpu.SemaphoreType.DMA(())],
        compiler_params=_params(("arbitrary",)),
    )(tok.reshape(n_tiles, 1, MOE_TILE), xn)


def _combine_kernel(s1_ref, s2_ref, h_ref, g1_ref, g2_ref, fg_ref, y_hbm, o_ref, a_ref, b_ref, sem):
    n = a_ref.shape[0]

    def start(r, carry):
        _row_copy(y_hbm, a_ref, sem, s1_ref[0, 0, r], r).start()
        _row_copy(y_hbm, b_ref, sem, s2_ref[0, 0, r], r).start()
        return carry

    lax.fori_loop(0, n, start, 0)

    def wait(r, carry):
        _row_copy(y_hbm, a_ref, sem, 0, r).wait()
        _row_copy(y_hbm, b_ref, sem, 0, r).wait()
        return carry

    lax.fori_loop(0, n, wait, 0)
    h = h_ref[...] + g1_ref[...] * a_ref[...] + g2_ref[...] * b_ref[...]
    ms = jnp.mean(h * h, axis=-1, keepdims=True)
    o_ref[...] = h * lax.rsqrt(ms + EPS) * fg_ref[...]


def _moe_combine_final(h3, y, slot1, slot2, g1, g2, final_g, n_real, tm=256):
    b, _, d = h3.shape
    nt = n_real // tm
    idx = lambda a: a.reshape(b * nt, 1, tm)
    gate = lambda a: a.reshape(b, n_real, 1)
    smem_blk = pl.BlockSpec((1, 1, tm), lambda bi, i: (bi * nt + i, 0, 0), memory_space=pltpu.SMEM)
    return pl.pallas_call(
        _combine_kernel,
        grid=(b, nt),
        in_specs=[smem_blk, smem_blk,
                  pl.BlockSpec((None, tm, d), lambda bi, i: (bi, i, 0)),
                  pl.BlockSpec((None, tm, 1), lambda bi, i: (bi, i, 0)),
                  pl.BlockSpec((None, tm, 1), lambda bi, i: (bi, i, 0)),
                  pl.BlockSpec((1, d), lambda bi, i: (0, 0)),
                  pl.BlockSpec(memory_space=pl.ANY)],
        out_specs=pl.BlockSpec((None, tm, d), lambda bi, i: (bi, i, 0)),
        out_shape=jax.ShapeDtypeStruct((b, n_real, d), F32),
        scratch_shapes=[pltpu.VMEM((tm, d), F32), pltpu.VMEM((tm, d), F32), pltpu.SemaphoreType.DMA(())],
        compiler_params=_params(("arbitrary", "arbitrary")),
    )(idx(slot1), idx(slot2), h3, gate(g1), gate(g2), final_g.reshape(1, d), y)


def _filter_features(l, lp, n_meta, n_emb):
    r = np.arange(lp)
    pos = (r + n_meta) % l
    valid = r < l
    t = pos / (l - 1.0)
    bands = (n_emb - 1) // 2
    w = 2.0 * np.pi * pos / l
    f = np.linspace(1e-4, bands - 1, bands)
    z = np.zeros((lp, V7X_LANES), np.float64)
    z[:, 0] = t
    z[:, 1:1 + bands] = np.cos(f[None, :] * w[:, None])
    z[:, 1 + bands:1 + 2 * bands] = -np.sin(f[None, :] * w[:, None])
    z *= valid[:, None]
    z[:, V7X_LANES - 1] = valid
    return jnp.asarray(z, F32)


def kernel(x, meta_tokens, norm_mix_g, w_in, hy_conv_w, hy_conv_b, hy_f_w1, hy_f_b1, hy_f_w2, hy_f_b2, hy_f_w3, hy_f_b3, hy_f_wo, hy_f_freq, hy_decay, hy_skip, cv_dw_w, cv_dw_b, cv_ln_g, cv_ln_b, attn_sink, group_norm_g, w_out, norm_ffn_g, ffn_w1, ffn_w3, ffn_w2, router_w, router_b, moe_w1, moe_w3, moe_w2, final_norm_g):
    b, n_real, d = x.shape
    n_meta = meta_tokens.shape[0]
    depth = w_in.shape[0]
    l = n_real + n_meta
    m = b * l
    lp = _round_up(l, 3 * V7X_LANES)
    w_hy = hy_skip.shape[1]
    w_fn = w_hy
    w_cv = cv_dw_b.shape[1]
    kv_w = N_KV_HEADS * HEAD_DIM
    p_in = w_in.shape[2]
    w_at = p_in - 3 * w_hy - w_fn - 2 * w_cv - 2 * kv_w
    n_q_heads = w_at // HEAD_DIM
    d_ff = ffn_w1.shape[2]
    assert depth == 2 and router_w.shape[2] == N_EXPERTS
    assert n_meta % V7X_BF16_ROWS == 0 and n_real % 256 == 0 and l % 16 == 0
    off_fn = 3 * w_hy
    off_cv = off_fn + w_fn
    off_q = off_cv + 2 * w_cv
    off_k = off_q + w_at
    off_v = off_k + kv_w

    r = np.arange(lp)
    pos = (r + n_meta) % l
    time_valid = r < l
    freq = np.arange(lp)
    cos_f, sin_f = _dft_tables(pos, time_valid, np.pi / l, 2 * l, lp // 16, 0, lp)
    cos_i, sin_i = _dft_tables(freq, freq <= l, np.pi / l, 2 * l, l // 16,
                               (l - n_meta) // 16, lp)
    cos_l, sin_l = _dft_tables(pos, time_valid, 2 * np.pi / l, l, l // 16,
                               (l - n_meta) // 16, lp)
    wf = np.where(freq <= l, np.where((freq == 0) | (freq == l), 1.0, 2.0), 0.0) / (2.0 * l)
    wf = jnp.asarray(wf.reshape(lp, 1), F32)
    zc = _filter_features(l, lp, n_meta, hy_f_w1.shape[1])

    h = jnp.concatenate([x, jnp.broadcast_to(meta_tokens[None].astype(x.dtype), (b, n_meta, d))], axis=1)
    h = h.reshape(m, d)
    out = None
    for layer in range(depth):
        xn = _rmsnorm(h, norm_mix_g[layer], BF16)
        proj = _matmul(xn, [w_in[layer]], tk=d, k_blk=0, n_out=p_in, tn=512, tm=1040, out_dtype=F32)
        proj3 = proj.reshape(b, l, p_in)

        hs, hd = _hyena_filter_sums(zc, hy_f_w1[layer], hy_f_b1[layer], hy_f_w2[layer], hy_f_b2[layer],
                                    hy_f_w3[layer], hy_f_b3[layer], hy_f_wo[layer], hy_f_freq[layer],
                                    hy_decay[layer], w_hy)
        kre, kim = _hyena_filter_spectrum(cos_f, sin_f, hs, hd, wf)
        x0c, vx = _hyena_prep(proj3, hy_conv_w[layer], hy_conv_b[layer], w_hy, n_real, n_meta, lp)
        yre, yim = _hyena_fwd(cos_f, sin_f, vx, kre, kim)
        y_a = _hyena_inv(cos_i, sin_i, yre, yim, vx, x0c, hy_skip[layer])

        fa, fb = _fnet_chan(proj3, off_fn // w_fn, w_fn, lp)
        y_b = _fnet_time(cos_l, sin_l, fa, fb, l)

        y_c = _conformer_conv(proj3, off_cv // 128, cv_dw_w[layer], cv_dw_b[layer], w_cv, n_real, n_meta)
        y_d = _windowed_attention(proj3, attn_sink[layer], off_q, off_k, off_v, n_q_heads, n_real, n_meta)

        ycat = _group_norms(y_a.reshape(m, w_hy), y_b.reshape(m, w_fn), y_c.reshape(m, w_cv),
                            y_d.reshape(m, w_at), group_norm_g[layer], cv_ln_g[layer], cv_ln_b[layer])
        h = _matmul(ycat, [w_out[layer]], tk=ycat.shape[1], k_blk=0, n_out=d, tn=512, tm=1040,
                    out_dtype=F32, residual=h)

        if layer % 2 == 0:
            i = layer // 2
            xn = _rmsnorm(h, norm_ffn_g[layer], BF16)
            tn_ff = _pick_tile(d_ff, (512, 256, 128))
            act = _matmul(xn, [ffn_w1[i], ffn_w3[i]], tk=d, k_blk=0, n_out=d_ff, tn=tn_ff, tm=1040,
                          out_dtype=BF16, swiglu=True)
            n_k = 2 if d_ff % (2 * V7X_LANES) == 0 and d_ff > 4096 else 1
            for kb in range(n_k):
                h = _matmul(act, [ffn_w2[i]], tk=d_ff // n_k, k_blk=kb, n_out=d, tn=256, tm=1040,
                            out_dtype=F32, residual=h)
        else:
            i = layer // 2
            h3 = h.reshape(b, l, d)
            mr = b * n_real
            xr = _rmsnorm_real(h3, norm_ffn_g[layer], n_real).reshape(mr, d)
            logits_t = _router_logits(xr, router_w[i])
            n_slots = _round_up(2 * mr + N_EXPERTS * (MOE_TILE - 1), MOE_TILE)
            n_tiles = n_slots // MOE_TILE
            slot1, slot2, g1, g2, te, tv = _route(logits_t, router_b[i], n_tiles)
            slot1, slot2 = slot1.reshape(mr), slot2.reshape(mr)
            te, tv = te.reshape(-1)[:n_tiles], tv.reshape(-1)[:n_tiles]
            tok = _invert_slots(slot1, slot2, n_slots)
            xs = _gather_rows(xr, tok)
            act = _grouped_matmul(xs, [moe_w1[i], moe_w3[i]], te, tv, tn=512, out_dtype=BF16, swiglu=True)
            ys = _grouped_matmul(act, [moe_w2[i]], te, tv, tn=512, out_dtype=F32)
            out = _moe_combine_final(h3, ys, slot1, slot2, g1, g2, final_norm_g, n_real)
    return out
```

```python
import functools
import math

import numpy as np
import jax
import jax.numpy as jnp
from jax import lax
from jax.experimental import pallas as pl
from jax.experimental.pallas import tpu as pltpu

F32 = jnp.float32
BF16 = jnp.bfloat16

V7X_LANES = 128
V7X_BF16_ROWS = 16
V7X_VMEM_LIMIT = 58 * 1024 * 1024

EPS = 1e-6
NEG = -1e30
HEAD_DIM = 128
N_KV_HEADS = 2
ATT_BLOCK = 128
FNET_GROUP_W = 256
N_EXPERTS = 8
MOE_TILE = 256
ROW_DMA_UNROLL = 4


def _round_up(x, m):
    return (x + m - 1) // m * m


def _pick_tile(n, candidates):
    for c in candidates:
        if n % c == 0:
            return c
    raise ValueError(f"no tile for {n} in {candidates}")


def _params(sem, vmem=None):
    if sem is None:
        return pltpu.CompilerParams(vmem_limit_bytes=vmem or V7X_VMEM_LIMIT)
    return pltpu.CompilerParams(dimension_semantics=sem, vmem_limit_bytes=vmem or V7X_VMEM_LIMIT)


def _rmsnorm_kernel(x_ref, g_ref, o_ref):
    x = x_ref[...]
    ms = jnp.mean(x * x, axis=-1, keepdims=True)
    o_ref[...] = (x * lax.rsqrt(ms + EPS) * g_ref[...]).astype(o_ref.dtype)


def _rmsnorm(x2d, g, out_dtype, tm=256):
    m, d = x2d.shape
    return pl.pallas_call(
        _rmsnorm_kernel,
        grid=(pl.cdiv(m, tm),),
        in_specs=[pl.BlockSpec((tm, d), lambda i: (i, 0)), pl.BlockSpec((1, d), lambda i: (0, 0))],
        out_specs=pl.BlockSpec((tm, d), lambda i: (i, 0)),
        out_shape=jax.ShapeDtypeStruct((m, d), out_dtype),
        compiler_params=_params(("arbitrary",)),
    )(x2d, g.reshape(1, d))


def _cast_tile(w_ref, wb_ref):
    k = w_ref.shape[0]
    ck = _pick_tile(k, (256, 128, 64, 32, 16))

    def body(c, carry):
        r = pl.multiple_of(c * ck, ck)
        wb_ref[pl.ds(r, ck), :] = w_ref[pl.ds(r, ck), :].astype(BF16)
        return carry

    lax.fori_loop(0, k // ck, body, 0)


def _mm_kernel(*refs, n_w, swiglu, has_res):
    x_ref = refs[0]
    w_refs = refs[1:1 + n_w]
    pos = 1 + n_w
    res_ref = refs[pos] if has_res else None
    pos += int(has_res)
    o_ref = refs[pos]
    wb_refs = refs[pos + 1:pos + 1 + n_w]

    @pl.when(pl.program_id(1) == 0)
    def _():
        for w_ref, wb_ref in zip(w_refs, wb_refs):
            _cast_tile(w_ref, wb_ref)

    x = x_ref[...]
    acc = jnp.dot(x, wb_refs[0][...], preferred_element_type=F32)
    if swiglu:
        up = jnp.dot(x, wb_refs[1][...], preferred_element_type=F32)
        acc = acc * jax.nn.sigmoid(acc) * up
    if has_res:
        acc = acc + res_ref[...]
    o_ref[...] = acc.astype(o_ref.dtype)


def _matmul(x, ws, layer, *, tk, k_blk, n_out, tn, tm, out_dtype, residual=None, swiglu=False):
    m = x.shape[0]
    n_w = len(ws)
    tm = min(tm, _round_up(m, V7X_BF16_ROWS))
    in_specs = [pl.BlockSpec((tm, tk), lambda j, i: (i, k_blk))]
    in_specs += [pl.BlockSpec((None, tk, tn), lambda j, i: (layer, k_blk, j)) for _ in ws]
    args = [x, *ws]
    if residual is not None:
        in_specs.append(pl.BlockSpec((tm, tn), lambda j, i: (i, j)))
        args.append(residual)
    return pl.pallas_call(
        functools.partial(_mm_kernel, n_w=n_w, swiglu=swiglu, has_res=residual is not None),
        grid=(n_out // tn, pl.cdiv(m, tm)),
        in_specs=in_specs,
        out_specs=pl.BlockSpec((tm, tn), lambda j, i: (i, j)),
        out_shape=jax.ShapeDtypeStruct((m, n_out), out_dtype),
        scratch_shapes=[pltpu.VMEM((tk, tn), BF16) for _ in ws],
        compiler_params=_params(("arbitrary", "arbitrary")),
    )(*args)


def _gmm_kernel(te_ref, tv_ref, nx_ref, x_ref, *refs, n_w, swiglu, tn):
    w_hbm = refs[:n_w]
    o_ref, stage_ref, wb_ref, sem = refs[n_w:n_w + 4]
    j = pl.program_id(0)
    t = pl.program_id(1)
    n_j = pl.num_programs(0)
    e = te_ref[t]

    def w_copy(i, expert, col_tile):
        col = pl.multiple_of(col_tile * tn, tn)
        return pltpu.make_async_copy(w_hbm[i].at[expert, :, pl.ds(col, tn)], stage_ref.at[i], sem.at[i])

    @pl.when((j == 0) & (t == 0))
    def _():
        for i in range(n_w):
            w_copy(i, e, 0).start()

    @pl.when((t == 0) | (e != te_ref[jnp.maximum(t - 1, 0)]))
    def _():
        for i in range(n_w):
            w_copy(i, e, j).wait()
            _cast_tile(stage_ref.at[i], wb_ref.at[i])
        e_next = nx_ref[t]

        @pl.when(e_next < N_EXPERTS)
        def _():
            for i in range(n_w):
                w_copy(i, e_next, j).start()

        @pl.when((e_next >= N_EXPERTS) & (j + 1 < n_j))
        def _():
            for i in range(n_w):
                w_copy(i, te_ref[0], j + 1).start()

    @pl.when(tv_ref[t] != 0)
    def _():
        x = x_ref[...].astype(BF16)
        acc = jnp.dot(x, wb_ref[0], preferred_element_type=F32)
        if swiglu:
            acc = acc * jax.nn.sigmoid(acc) * jnp.dot(x, wb_ref[1], preferred_element_type=F32)
        o_ref[...] = acc.astype(o_ref.dtype)

    @pl.when(tv_ref[t] == 0)
    def _():
        o_ref[...] = jnp.zeros_like(o_ref)


def _grouped_matmul(x, ws, tile_expert, tile_valid, next_expert, *, tn, out_dtype, swiglu=False):
    s, k = x.shape
    n = ws[0].shape[2]
    n_w = len(ws)
    grid_spec = pltpu.PrefetchScalarGridSpec(
        num_scalar_prefetch=3,
        grid=(n // tn, s // MOE_TILE),
        in_specs=[pl.BlockSpec((MOE_TILE, k), lambda j, t, te, tv, nx: (t, 0))]
        + [pl.BlockSpec(memory_space=pl.ANY) for _ in ws],
        out_specs=pl.BlockSpec((MOE_TILE, tn), lambda j, t, te, tv, nx: (t, j)),
        scratch_shapes=[pltpu.VMEM((n_w, k, tn), F32), pltpu.VMEM((n_w, k, tn), BF16),
                        pltpu.SemaphoreType.DMA((n_w,))],
    )
    return pl.pallas_call(
        functools.partial(_gmm_kernel, n_w=n_w, swiglu=swiglu, tn=tn),
        grid_spec=grid_spec,
        out_shape=jax.ShapeDtypeStruct((s, n), out_dtype),
        compiler_params=_params(("arbitrary", "arbitrary")),
    )(tile_expert, tile_valid, next_expert, x, *ws)


def _dft_table_kernel(ac_ref, as_ref, bc_ref, bs_ref, cc_ref, cs_ref, c_ref, s_ref, *,
                      n_blocks, shift, n_rows_out):
    ac = ac_ref[...]
    asn = as_ref[...]

    def body(q, carry):
        v = q % 16
        u = q // 16
        bc, bs = bc_ref[pl.ds(v, 1), :], bs_ref[pl.ds(v, 1), :]
        cc, cs = cc_ref[pl.ds(u, 1), :], cs_ref[pl.ds(u, 1), :]
        dc = bc * cc - bs * cs
        dsn = bs * cc + bc * cs
        ob = q + shift
        ob = jnp.where(ob >= n_blocks, ob - n_blocks, ob)
        row = pl.multiple_of(ob * 16, 16)
        c_ref[pl.ds(row, 16), :] = (ac * dc - asn * dsn).astype(BF16)
        s_ref[pl.ds(row, 16), :] = (asn * dc + ac * dsn).astype(BF16)
        return carry

    lax.fori_loop(0, n_blocks, body, 0)
    pad = n_rows_out - 16 * n_blocks
    if pad:
        c_ref[pl.ds(16 * n_blocks, pad), :] = jnp.zeros((pad, c_ref.shape[1]), BF16)
        s_ref[pl.ds(16 * n_blocks, pad), :] = jnp.zeros((pad, s_ref.shape[1]), BF16)


def _dft_tables(col_pos, col_valid, unit, period, n_blocks, shift, n_rows_out):
    ncol = col_pos.shape[0]
    n_u = -(-n_blocks // 16)
    p = col_pos.astype(np.int64)[None, :]
    valid = col_valid[None, :].astype(np.float64)

    def factor(rows, stride, mask):
        ang = unit * ((np.arange(rows, dtype=np.int64)[:, None] * stride * p) % period)
        return (jnp.asarray((np.cos(ang) * mask).astype(np.float32)),
                jnp.asarray((np.sin(ang) * mask).astype(np.float32)))

    a_c, a_s = factor(16, 1, valid)
    b_c, b_s = factor(16, 16, 1.0)
    c_c, c_s = factor(n_u, 256, 1.0)
    pc = _pick_tile(ncol, (384, 256, 128))
    kern = functools.partial(_dft_table_kernel, n_blocks=n_blocks, shift=shift, n_rows_out=n_rows_out)
    col = lambda rows: pl.BlockSpec((rows, pc), lambda j: (0, j))
    return pl.pallas_call(
        kern,
        grid=(ncol // pc,),
        in_specs=[col(16), col(16), col(16), col(16), col(n_u), col(n_u)],
        out_specs=[col(n_rows_out), col(n_rows_out)],
        out_shape=[jax.ShapeDtypeStruct((n_rows_out, ncol), BF16)] * 2,
        compiler_params=_params(("arbitrary",)),
    )(a_c, a_s, b_c, b_s, c_c, c_s)


def _filter_kernel(z_ref, w1_ref, b1_ref, w2_ref, b2_ref, w3_ref, b3_ref, wo_ref, fr_ref, dec_ref,
                   hs_ref, hd_ref, *, width):
    hp = lax.Precision.HIGHEST
    z = z_ref[...]
    fr = fr_ref[...]
    hid = jnp.sin(fr * (jnp.dot(z, w1_ref[...], precision=hp, preferred_element_type=F32) + b1_ref[...]))
    hid = jnp.sin(fr * (jnp.dot(hid, w2_ref[...], precision=hp, preferred_element_type=F32) + b2_ref[...]))
    hid = jnp.sin(fr * (jnp.dot(hid, w3_ref[...], precision=hp, preferred_element_type=F32) + b3_ref[...]))
    h = jnp.dot(hid, wo_ref[...], precision=hp, preferred_element_type=F32)
    t = z[:, 0:1]
    valid = z[:, V7X_LANES - 1:V7X_LANES]
    h = h * jnp.exp(-t * dec_ref[...]) * valid
    hf = h[:, :width]
    hb = h[:, width:]
    hs_ref[...] = (hf + hb).astype(BF16)
    hd_ref[...] = (hf - hb).astype(BF16)


def _hyena_filter_sums(zc, w1, b1, w2, b2, w3, b3, wo, freq, decay, width):
    lp = zc.shape[0]
    hid = w2.shape[0]
    tr = _pick_tile(lp, (528, 384, 256, 128))
    w1p = jnp.zeros((V7X_LANES, hid), F32).at[:w1.shape[0]].set(w1)
    full = lambda shape: pl.BlockSpec(shape, lambda i: (0, 0))
    return pl.pallas_call(
        functools.partial(_filter_kernel, width=width),
        grid=(lp // tr,),
        in_specs=[pl.BlockSpec((tr, V7X_LANES), lambda i: (i, 0)),
                  full((V7X_LANES, hid)), full((1, hid)), full((hid, hid)), full((1, hid)),
                  full((hid, hid)), full((1, hid)), full((hid, 2 * width)), full((1, hid)),
                  full((1, 2 * width))],
        out_specs=[pl.BlockSpec((tr, width), lambda i: (i, 0))] * 2,
        out_shape=[jax.ShapeDtypeStruct((lp, width), BF16)] * 2,
        compiler_params=_params(("arbitrary",)),
    )(zc, w1p, b1.reshape(1, hid), w2, b2.reshape(1, hid), w3, b3.reshape(1, hid), wo,
      freq.reshape(1, hid), decay.reshape(1, 2 * width))


def _kf_kernel(c_ref, s_ref, hs_ref, hd_ref, wf_ref, kre_ref, kim_ref):
    wf = wf_ref[...]
    kre_ref[...] = wf * jnp.dot(c_ref[...], hs_ref[...], preferred_element_type=F32)
    kim_ref[...] = -wf * jnp.dot(s_ref[...], hd_ref[...], preferred_element_type=F32)


def _hyena_filter_spectrum(cos_t, sin_t, hs, hd, wf):
    fp, lp = cos_t.shape
    width = hs.shape[1]
    tf = _pick_tile(fp, (384, 256, 128))
    cb = 512
    return pl.pallas_call(
        _kf_kernel,
        grid=(width // cb, fp // tf),
        in_specs=[pl.BlockSpec((tf, lp), lambda c, f: (f, 0)), pl.BlockSpec((tf, lp), lambda c, f: (f, 0)),
                  pl.BlockSpec((lp, cb), lambda c, f: (0, c)), pl.BlockSpec((lp, cb), lambda c, f: (0, c)),
                  pl.BlockSpec((tf, 1), lambda c, f: (f, 0))],
        out_specs=[pl.BlockSpec((tf, cb), lambda c, f: (f, c))] * 2,
        out_shape=[jax.ShapeDtypeStruct((fp, width), F32)] * 2,
        compiler_params=_params(("arbitrary", "arbitrary")),
    )(cos_t, sin_t, hs, hd, wf)


def _hyena_prep_kernel(x0_ref, x1_ref, v_ref, w0_ref, w1_ref, wv_ref, b0_ref, b1_ref, bv_ref,
                       x0c_ref, vx_ref, pad_ref, *, n_real, n_meta, lp):
    cb = pad_ref.shape[1]
    l = n_real + n_meta
    pad_ref[pl.ds(0, 8), :] = jnp.zeros((8, cb), F32)
    pad_ref[pl.ds(8 + l, 8), :] = jnp.zeros((8, cb), F32)

    def conv(u_ref, w_ref, b_ref):
        pad_ref[pl.ds(8, n_meta), :] = u_ref[pl.ds(n_real, n_meta), :]
        pad_ref[pl.ds(8 + n_meta, n_real), :] = u_ref[pl.ds(0, n_real), :]
        w = w_ref[...]
        b = b_ref[...]

        def at(start, n):
            return (w[0:1] * pad_ref[pl.ds(start - 1, n), :] + w[1:2] * pad_ref[pl.ds(start, n), :]
                    + w[2:3] * pad_ref[pl.ds(start + 1, n), :] + b)

        return at(8 + n_meta, n_real), at(8, n_meta)

    x0_real, x0_meta = conv(x0_ref, w0_ref, b0_ref)
    x0c_ref[pl.ds(0, n_real), :] = x0_real
    x0c_ref[pl.ds(n_real, n_meta), :] = x0_meta
    x1_real, x1_meta = conv(x1_ref, w1_ref, b1_ref)
    v_real, v_meta = conv(v_ref, wv_ref, bv_ref)
    vx_ref[pl.ds(0, n_real), :] = (v_real * x1_real).astype(BF16)
    vx_ref[pl.ds(n_real, n_meta), :] = (v_meta * x1_meta).astype(BF16)
    vx_ref[pl.ds(l, lp - l), :] = jnp.zeros((lp - l, cb), BF16)


def _hyena_prep(proj3, conv_w, conv_b, width, n_real, n_meta, lp):
    b, l, _ = proj3.shape
    cb = 128
    nc = width // cb
    col = lambda off: pl.BlockSpec((None, l, cb), lambda bi, c: (bi, 0, off + c))
    wcol = lambda off: pl.BlockSpec((3, cb), lambda bi, c: (0, off + c))
    bcol = lambda off: pl.BlockSpec((1, cb), lambda bi, c: (0, off + c))
    cbias = conv_b.reshape(1, 3 * width)
    return pl.pallas_call(
        functools.partial(_hyena_prep_kernel, n_real=n_real, n_meta=n_meta, lp=lp),
        grid=(b, nc),
        in_specs=[col(0), col(nc), col(2 * nc), wcol(0), wcol(nc), wcol(2 * nc),
                  bcol(0), bcol(nc), bcol(2 * nc)],
        out_specs=[pl.BlockSpec((None, l, cb), lambda bi, c: (bi, 0, c)),
                   pl.BlockSpec((None, lp, cb), lambda bi, c: (bi, 0, c))],
        out_shape=[jax.ShapeDtypeStruct((b, l, width), F32), jax.ShapeDtypeStruct((b, lp, width), BF16)],
        scratch_shapes=[pltpu.VMEM((l + 16, cb), F32)],
        compiler_params=_params(("arbitrary", "arbitrary")),
    )(proj3, proj3, proj3, conv_w, conv_w, conv_w, cbias, cbias, cbias)


def _hyena_fwd_kernel(c_ref, s_ref, vx_ref, kre_ref, kim_ref, yre_ref, yim_ref):
    vx = vx_ref[...]
    ure = jnp.dot(c_ref[...], vx, preferred_element_type=F32)
    uim = jnp.dot(s_ref[...], vx, preferred_element_type=F32)
    kre = kre_ref[...]
    kim = kim_ref[...]
    yre_ref[...] = (ure * kre + uim * kim).astype(BF16)
    yim_ref[...] = (ure * kim - uim * kre).astype(BF16)


def _hyena_fwd(cos_t, sin_t, vx, kre, kim):
    fp, lp = cos_t.shape
    b, _, width = vx.shape
    tf = _pick_tile(fp, (384, 256, 128))
    cb = 512
    return pl.pallas_call(
        _hyena_fwd_kernel,
        grid=(width // cb, b, fp // tf),
        in_specs=[pl.BlockSpec((tf, lp), lambda c, bi, f: (f, 0)),
                  pl.BlockSpec((tf, lp), lambda c, bi, f: (f, 0)),
                  pl.BlockSpec((None, lp, cb), lambda c, bi, f: (bi, 0, c)),
                  pl.BlockSpec((tf, cb), lambda c, bi, f: (f, c)),
                  pl.BlockSpec((tf, cb), lambda c, bi, f: (f, c))],
        out_specs=[pl.BlockSpec((None, tf, cb), lambda c, bi, f: (bi, f, c))] * 2,
        out_shape=[jax.ShapeDtypeStruct((b, fp, width), BF16)] * 2,
        compiler_params=_params(("arbitrary", "arbitrary", "arbitrary")),
    )(cos_t, sin_t, vx, kre, kim)


def _hyena_inv_kernel(ct_ref, st_ref, yre_ref, yim_ref, vx_ref, x0c_ref, skip_ref, o_ref):
    y = (jnp.dot(ct_ref[...], yre_ref[...], preferred_element_type=F32)
         - jnp.dot(st_ref[...], yim_ref[...], preferred_element_type=F32))
    o_ref[...] = (y + skip_ref[...] * vx_ref[...].astype(F32)) * x0c_ref[...]


def _hyena_inv(cos_tt, sin_tt, yre, yim, vx, x0c, skip):
    lp, fp = cos_tt.shape
    b, l, width = x0c.shape
    tr = _pick_tile(lp, (384, 256, 128))
    cb = 512
    return pl.pallas_call(
        _hyena_inv_kernel,
        grid=(width // cb, b, lp // tr),
        in_specs=[pl.BlockSpec((tr, fp), lambda c, bi, r: (r, 0)),
                  pl.BlockSpec((tr, fp), lambda c, bi, r: (r, 0)),
                  pl.BlockSpec((None, fp, cb), lambda c, bi, r: (bi, 0, c)),
                  pl.BlockSpec((None, fp, cb), lambda c, bi, r: (bi, 0, c)),
                  pl.BlockSpec((None, tr, cb), lambda c, bi, r: (bi, r, c)),
                  pl.BlockSpec((None, tr, cb), lambda c, bi, r: (bi, r, c)),
                  pl.BlockSpec((1, cb), lambda c, bi, r: (0, c))],
        out_specs=pl.BlockSpec((None, tr, cb), lambda c, bi, r: (bi, r, c)),
        out_shape=jax.ShapeDtypeStruct((b, l, width), F32),
        compiler_params=_params(("arbitrary", "arbitrary", "arbitrary")),
    )(cos_tt, sin_tt, yre, yim, vx, x0c, skip.reshape(1, width))


def _fnet_chan_kernel(u_ref, cc_ref, sc_ref, a_ref, b_ref, *, l, tr):
    row = pl.program_id(1) * tr + lax.broadcasted_iota(jnp.int32, (tr, 1), 0)
    u = jnp.where(row < l, u_ref[...], 0.0).astype(BF16)
    gw = FNET_GROUP_W
    for g in range(u.shape[1] // gw):
        ug = u[:, g * gw:(g + 1) * gw]
        a_ref[:, g * gw:(g + 1) * gw] = jnp.dot(ug, cc_ref[...], preferred_element_type=F32).astype(BF16)
        b_ref[:, g * gw:(g + 1) * gw] = jnp.dot(ug, sc_ref[...], preferred_element_type=F32).astype(BF16)


def _fnet_chan(proj3, col_blk, width, lp):
    b, l, _ = proj3.shape
    tr = _pick_tile(lp, (384, 256, 128))
    k = np.arange(FNET_GROUP_W)
    ang = 2.0 * np.pi * ((k[:, None] * k[None, :]) % FNET_GROUP_W) / FNET_GROUP_W
    cc = jnp.asarray(np.cos(ang), BF16)
    sc = jnp.asarray(np.sin(ang), BF16)
    return pl.pallas_call(
        functools.partial(_fnet_chan_kernel, l=l, tr=tr),
        grid=(b, lp // tr),
        in_specs=[pl.BlockSpec((None, tr, width), lambda bi, r: (bi, r, col_blk)),
                  pl.BlockSpec((FNET_GROUP_W, FNET_GROUP_W), lambda bi, r: (0, 0)),
                  pl.BlockSpec((FNET_GROUP_W, FNET_GROUP_W), lambda bi, r: (0, 0))],
        out_specs=[pl.BlockSpec((None, tr, width), lambda bi, r: (bi, r, 0))] * 2,
        out_shape=[jax.ShapeDtypeStruct((b, lp, width), BF16)] * 2,
        compiler_params=_params(("arbitrary", "arbitrary")),
    )(proj3, cc, sc)


def _fnet_time_kernel(c_ref, s_ref, a_ref, b_ref, o_ref, *, scale):
    y = (jnp.dot(c_ref[...], a_ref[...], preferred_element_type=F32)
         - jnp.dot(s_ref[...], b_ref[...], preferred_element_type=F32))
    o_ref[...] = y * scale


def _fnet_time(cos_l, sin_l, a, bm, l):
    lp = cos_l.shape[0]
    b, _, width = a.shape
    tr = _pick_tile(lp, (384, 256, 128))
    cb = 512
    scale = 1.0 / math.sqrt(l * FNET_GROUP_W)
    return pl.pallas_call(
        functools.partial(_fnet_time_kernel, scale=scale),
        grid=(width // cb, b, lp // tr),
        in_specs=[pl.BlockSpec((tr, lp), lambda c, bi, r: (r, 0)),
                  pl.BlockSpec((tr, lp), lambda c, bi, r: (r, 0)),
                  pl.BlockSpec((None, lp, cb), lambda c, bi, r: (bi, 0, c)),
                  pl.BlockSpec((None, lp, cb), lambda c, bi, r: (bi, 0, c))],
        out_specs=pl.BlockSpec((None, tr, cb), lambda c, bi, r: (bi, r, c)),
        out_shape=jax.ShapeDtypeStruct((b, l, width), F32),
        compiler_params=_params(("arbitrary", "arbitrary", "arbitrary")),
    )(cos_l, sin_l, a, bm)


def _conformer_kernel(a_ref, g_ref, w_ref, b_ref, o_ref, pad_ref, *, n_real, n_meta, taps, chunk):
    cb = pad_ref.shape[1]
    half = taps // 2
    lead = _round_up(half, 8)
    l = n_real + n_meta
    pad_ref[pl.ds(0, lead), :] = jnp.zeros((lead, cb), F32)
    pad_ref[pl.ds(lead + l, lead), :] = jnp.zeros((lead, cb), F32)
    pad_ref[pl.ds(lead, n_meta), :] = (a_ref[pl.ds(n_real, n_meta), :]
                                       * jax.nn.sigmoid(g_ref[pl.ds(n_real, n_meta), :]))
    pad_ref[pl.ds(lead + n_meta, n_real), :] = (a_ref[pl.ds(0, n_real), :]
                                                * jax.nn.sigmoid(g_ref[pl.ds(0, n_real), :]))
    w = w_ref[...]
    bias = b_ref[...]

    def window(start, n):
        acc = jnp.broadcast_to(bias, (n, cb))
        for j in range(taps):
            acc = acc + w[j:j + 1] * pad_ref[pl.ds(start + j - half, n), :]
        return acc

    def body(ci, carry):
        base = pl.multiple_of(ci * chunk, chunk)
        o_ref[pl.ds(base, chunk), :] = window(base + lead + n_meta, chunk)
        return carry

    lax.fori_loop(0, n_real // chunk, body, 0)
    o_ref[pl.ds(n_real, n_meta), :] = window(lead, n_meta)


def _conformer_conv(proj3, col_blk, dw_w, dw_b, width, n_real, n_meta):
    b, l, _ = proj3.shape
    taps = dw_w.shape[0]
    cb = 128
    nc = width // cb
    lead = _round_up(taps // 2, 8)
    return pl.pallas_call(
        functools.partial(_conformer_kernel, n_real=n_real, n_meta=n_meta, taps=taps, chunk=64),
        grid=(b, nc),
        in_specs=[pl.BlockSpec((None, l, cb), lambda bi, c: (bi, 0, col_blk + c)),
                  pl.BlockSpec((None, l, cb), lambda bi, c: (bi, 0, col_blk + nc + c)),
                  pl.BlockSpec((taps, cb), lambda bi, c: (0, c)),
                  pl.BlockSpec((1, cb), lambda bi, c: (0, c))],
        out_specs=pl.BlockSpec((None, l, cb), lambda bi, c: (bi, 0, c)),
        out_shape=jax.ShapeDtypeStruct((b, l, width), F32),
        scratch_shapes=[pltpu.VMEM((l + 2 * lead, cb), F32)],
        compiler_params=_params(("arbitrary", "arbitrary")),
    )(proj3, proj3, dw_w, dw_b.reshape(1, width))


def _softmax_pv(parts, sink, out_dtype=F32):
    m = sink
    for s, _ in parts:
        m = jnp.maximum(m, jnp.max(s, axis=-1, keepdims=True))
    denom = jnp.exp(sink - m)
    acc = None
    for s, v in parts:
        e = jnp.exp(s - m)
        denom = denom + jnp.sum(e, axis=-1, keepdims=True)
        pv = jnp.dot(e.astype(BF16), v, preferred_element_type=F32)
        acc = pv if acc is None else acc + pv
    return (acc / denom).astype(out_dtype)


def _qk(q, k):
    return lax.dot_general(q, k, (((1,), (1,)), ((), ())), preferred_element_type=F32)


def _attn_kernel(slope_ref, sink_ref, q_ref, kp_ref, kc_ref, kn_ref, km_ref, vp_ref, vc_ref, vn_ref, vm_ref,
                 o_ref, *, nb, n_meta, q_per_kv):
    g = pl.program_id(1)
    i = pl.program_id(2)
    t = ATT_BLOCK
    rows = q_per_kv * t
    row = lax.broadcasted_iota(jnp.int32, (rows, t), 0)
    a = row & (t - 1)
    c = lax.broadcasted_iota(jnp.int32, (rows, t), 1)
    hrow = lax.broadcasted_iota(jnp.int32, (rows, 1), 0) // t
    slope = jnp.zeros((rows, 1), F32)
    sink = jnp.zeros((rows, 1), F32)
    for hh in range(q_per_kv):
        slope = jnp.where(hrow == hh, slope_ref[g * q_per_kv + hh], slope)
        sink = jnp.where(hrow == hh, sink_ref[g * q_per_kv + hh], sink)
    scale = 1.0 / math.sqrt(HEAD_DIM)
    q = jnp.concatenate([q_ref[:, hh * HEAD_DIM:(hh + 1) * HEAD_DIM] for hh in range(q_per_kv)], axis=0)
    q = (q * scale).astype(BF16)
    km = km_ref[...].astype(BF16)
    vm = vm_ref[...].astype(BF16)
    kn = kn_ref[...].astype(BF16)
    vn = vn_ref[...].astype(BF16)

    def store(o):
        o_ref[...] = jnp.concatenate([o[hh * t:(hh + 1) * t] for hh in range(q_per_kv)], axis=1)

    @pl.when(i < nb)
    def _():
        d_prev = (t + a - c).astype(F32)
        d_cur = jnp.abs(a - c).astype(F32)
        d_next = (t + c - a).astype(F32)
        ok_prev = c >= a + jnp.where(i > 0, 0, t)
        ok_next = c <= a - jnp.where(i < nb - 1, 0, t)
        kp, kc = kp_ref[...].astype(BF16), kc_ref[...].astype(BF16)
        vp, vc = vp_ref[...].astype(BF16), vc_ref[...].astype(BF16)
        s_m = _qk(q, km)
        s_p = jnp.where(ok_prev, _qk(q, kp) - slope * d_prev, NEG)
        s_c = _qk(q, kc) - slope * d_cur
        s_n = jnp.where(ok_next, _qk(q, kn) - slope * d_next, NEG)
        store(_softmax_pv([(s_m, vm), (s_p, vp), (s_c, vc), (s_n, vn)], sink))

    @pl.when(i == nb)
    def _():
        delta = n_meta + c - a
        s_m = _qk(q, km)
        s_0 = jnp.where(delta <= t, _qk(q, kn) - slope * delta.astype(F32), NEG)
        store(_softmax_pv([(s_m, vm), (s_0, vn)], sink))


def _windowed_attention(proj3, sink, q_col, k_col, v_col, n_q_heads, n_real, n_meta):
    b, l, _ = proj3.shape
    nb = n_real // ATT_BLOCK
    q_per_kv = n_q_heads // N_KV_HEADS
    qw = q_per_kv * HEAD_DIM
    qb, kb, vb = q_col // qw, k_col // HEAD_DIM, v_col // HEAD_DIM
    mb = n_real // n_meta
    slopes = jnp.asarray([2.0 ** (-8.0 * (h + 1) / n_q_heads) for h in range(n_q_heads)], F32)
    smem = pl.BlockSpec(memory_space=pltpu.SMEM)

    def kv_blk(colb, shift):
        def index(bi, g, i):
            blk = jnp.clip(i + shift, 0, nb - 1)
            if shift == 1:
                blk = jnp.where(i == nb, 0, blk)
            return bi, blk, colb + g
        return pl.BlockSpec((None, ATT_BLOCK, HEAD_DIM), index)

    kv_meta = lambda colb: pl.BlockSpec((None, n_meta, HEAD_DIM), lambda bi, g, i: (bi, mb, colb + g))
    return pl.pallas_call(
        functools.partial(_attn_kernel, nb=nb, n_meta=n_meta, q_per_kv=q_per_kv),
        grid=(b, N_KV_HEADS, nb + 1),
        in_specs=[smem, smem,
                  pl.BlockSpec((None, ATT_BLOCK, qw), lambda bi, g, i: (bi, i, qb + g)),
                  kv_blk(kb, -1), kv_blk(kb, 0), kv_blk(kb, 1), kv_meta(kb),
                  kv_blk(vb, -1), kv_blk(vb, 0), kv_blk(vb, 1), kv_meta(vb)],
        out_specs=pl.BlockSpec((None, ATT_BLOCK, qw), lambda bi, g, i: (bi, i, g)),
        out_shape=jax.ShapeDtypeStruct((b, l, n_q_heads * HEAD_DIM), F32),
        compiler_params=_params(("arbitrary", "arbitrary", "arbitrary")),
    )(slopes, sink, proj3, proj3, proj3, proj3, proj3, proj3, proj3, proj3, proj3)


def _group_norm_kernel(ya_ref, yb_ref, yc_ref, yd_ref, g_ref, lng_ref, lnb_ref, o_ref, *, widths):
    def rms(y, g):
        return y * lax.rsqrt(jnp.mean(y * y, axis=-1, keepdims=True) + EPS) * g

    off = 0
    for idx, (y_ref, w) in enumerate(zip((ya_ref, yb_ref, yc_ref, yd_ref), widths)):
        y = y_ref[...]
        if idx == 2:
            mu = jnp.mean(y, axis=-1, keepdims=True)
            yc = y - mu
            var = jnp.mean(yc * yc, axis=-1, keepdims=True)
            y = yc * lax.rsqrt(var + EPS) * lng_ref[...] + lnb_ref[...]
            y = y * jax.nn.sigmoid(y)
        o_ref[:, off:off + w] = rms(y, g_ref[:, off:off + w]).astype(o_ref.dtype)
        off += w


def _group_norms(ya, yb, yc, yd, group_g, ln_g, ln_b, tm=256):
    m = ya.shape[0]
    widths = (ya.shape[1], yb.shape[1], yc.shape[1], yd.shape[1])
    tot = sum(widths)
    row = lambda w: pl.BlockSpec((tm, w), lambda i: (i, 0))
    full = lambda w: pl.BlockSpec((1, w), lambda i: (0, 0))
    return pl.pallas_call(
        functools.partial(_group_norm_kernel, widths=widths),
        grid=(pl.cdiv(m, tm),),
        in_specs=[row(widths[0]), row(widths[1]), row(widths[2]), row(widths[3]),
                  full(tot), full(widths[2]), full(widths[2])],
        out_specs=row(tot),
        out_shape=jax.ShapeDtypeStruct((m, tot), BF16),
        compiler_params=_params(("arbitrary",)),
    )(ya, yb, yc, yd, group_g.reshape(1, tot), ln_g.reshape(1, -1), ln_b.reshape(1, -1))


def _norm_router_kernel(h_ref, g_ref, w_ref, o_ref):
    x = h_ref[...]
    ms = jnp.mean(x * x, axis=-1, keepdims=True)
    xn = (x * lax.rsqrt(ms + EPS) * g_ref[...]).astype(BF16)
    o_ref[...] = lax.dot_general(w_ref[...], xn, (((1,), (1,)), ((), ())), preferred_element_type=F32)


def _norm_router(h3, g, router_w, n_real, tm=256):
    b, _, d = h3.shape
    nt = n_real // tm
    wt = router_w.T.astype(BF16)
    return pl.pallas_call(
        _norm_router_kernel,
        grid=(b, nt),
        in_specs=[pl.BlockSpec((None, tm, d), lambda bi, i: (bi, i, 0)),
                  pl.BlockSpec((1, d), lambda bi, i: (0, 0)),
                  pl.BlockSpec((N_EXPERTS, d), lambda bi, i: (0, 0))],
        out_specs=pl.BlockSpec((N_EXPERTS, tm), lambda bi, i: (0, bi * nt + i)),
        out_shape=jax.ShapeDtypeStruct((N_EXPERTS, b * n_real), F32),
        compiler_params=_params(("arbitrary", "arbitrary")),
    )(h3, g.reshape(1, d), wt)


def _route_kernel(lg_ref, b_ref, slot1_ref, slot2_ref, g1_ref, g2_ref, te_ref, tv_ref, nx_ref, *, chunk):
    ne, m = lg_ref.shape
    logits = lg_ref[...] + b_ref[...]
    row = lax.broadcasted_iota(jnp.int32, (ne, m), 0)
    m1 = jnp.max(logits, axis=0, keepdims=True)
    e1 = jnp.min(jnp.where(logits == m1, row, ne), axis=0, keepdims=True)
    rest = jnp.where(row == e1, -jnp.inf, logits)
    m2 = jnp.max(rest, axis=0, keepdims=True)
    e2 = jnp.min(jnp.where(rest == m2, row, ne), axis=0, keepdims=True)
    g1 = 1.0 / (1.0 + jnp.exp(m2 - m1))
    g1_ref[...] = g1
    g2_ref[...] = 1.0 - g1
    oh1 = (row == e1).astype(F32)
    oh2 = (row == e2).astype(F32)
    c1 = jnp.sum(oh1, axis=1, keepdims=True)
    c2 = jnp.sum(oh2, axis=1, keepdims=True)
    padded = jnp.floor((c1 + c2 + (MOE_TILE - 1)) / MOE_TILE) * MOE_TILE
    erow = lax.broadcasted_iota(jnp.int32, (ne, 1), 0)
    start = jnp.zeros((ne, 1), F32)
    for e in range(ne - 1):
        start = start + jnp.where(erow > e, padded[e:e + 1, :], 0.0)
    end = start + padded
    tri = (lax.broadcasted_iota(jnp.int32, (chunk, chunk), 0)
           < lax.broadcasted_iota(jnp.int32, (chunk, chunk), 1)).astype(BF16)
    carry = jnp.concatenate([jnp.zeros((ne, 1), F32), c1], axis=0)
    base1 = start
    for ci in range(m // chunk):
        sl = slice(ci * chunk, (ci + 1) * chunk)
        oh = jnp.concatenate([oh1[:, sl], oh2[:, sl]], axis=0)
        pre = jnp.dot(oh.astype(BF16), tri, preferred_element_type=F32) + carry
        carry = carry + jnp.sum(oh, axis=1, keepdims=True)
        s1 = jnp.sum(oh[:ne] * (pre[:ne] + base1), axis=0, keepdims=True)
        s2 = jnp.sum(oh[ne:] * (pre[ne:] + base1), axis=0, keepdims=True)
        slot1_ref[:, sl] = s1.astype(jnp.int32)
        slot2_ref[:, sl] = s2.astype(jnp.int32)
    tile0 = (lax.broadcasted_iota(jnp.int32, (1, V7X_LANES), 1) * MOE_TILE).astype(F32)
    te = jnp.sum((end <= tile0).astype(jnp.int32), axis=0, keepdims=True)
    nonempty = padded > 0.0
    last = jnp.max(jnp.where(nonempty, erow, 0), axis=0, keepdims=True)
    te = jnp.minimum(te, last)
    te_ref[...] = te
    tv_ref[...] = (tile0 < end[ne - 1:ne, :]).astype(jnp.int32)
    nx_ref[...] = jnp.min(jnp.where(nonempty & (erow > te), erow, ne), axis=0, keepdims=True)


def _route(logits_t, router_b, n_tiles):
    ne, m = logits_t.shape
    assert n_tiles <= V7X_LANES
    vec = lambda dt: jax.ShapeDtypeStruct((1, m), dt)
    lane = jax.ShapeDtypeStruct((1, V7X_LANES), jnp.int32)
    return pl.pallas_call(
        functools.partial(_route_kernel, chunk=_pick_tile(m, (512, 256, 128))),
        out_shape=[vec(jnp.int32), vec(jnp.int32), vec(F32), vec(F32), lane, lane, lane],
        compiler_params=_params(None),
    )(logits_t, router_b.reshape(ne, 1))


def _row_copy(src_hbm, dst_vmem, sem, src_row, dst_row):
    return pltpu.make_async_copy(src_hbm.at[pl.ds(src_row, 1)], dst_vmem.at[pl.ds(dst_row, 1)], sem)


def _norm_scatter_kernel(s1_ref, s2_ref, h_ref, g_ref, xs_in_hbm, xs_hbm, pk_ref, sem, *, n_steps):
    del xs_in_hbm
    t = pl.program_id(0) * pl.num_programs(1) + pl.program_id(1)
    cur = t % 2
    tm = h_ref.shape[0]
    x = h_ref[...]
    ms = jnp.mean(x * x, axis=-1, keepdims=True)
    pk_ref[cur] = x * lax.rsqrt(ms + EPS) * g_ref[...]

    def row_out(buf, r, slot):
        return pltpu.make_async_copy(pk_ref.at[buf, pl.ds(r, 1)], xs_hbm.at[pl.ds(slot, 1)], sem.at[buf])

    def start(r4, carry):
        for k in range(ROW_DMA_UNROLL):
            r = r4 * ROW_DMA_UNROLL + k
            row_out(cur, r, s1_ref[0, 0, r]).start()
            row_out(cur, r, s2_ref[0, 0, r]).start()
        return carry

    lax.fori_loop(0, tm // ROW_DMA_UNROLL, start, 0)

    def wait_all(buf):
        def wait(r, carry):
            row_out(buf, r, 0).wait()
            row_out(buf, r, 0).wait()
            return carry

        lax.fori_loop(0, tm, wait, 0)

    @pl.when(t > 0)
    def _():
        wait_all(1 - cur)

    @pl.when(t == n_steps - 1)
    def _():
        wait_all(cur)


def _norm_scatter(h3, g, slot1, slot2, n_real, n_slots, tm=256):
    b, _, d = h3.shape
    nt = n_real // tm
    idx = lambda a: a.reshape(b * nt, 1, tm)
    smem_blk = pl.BlockSpec((1, 1, tm), lambda bi, i: (bi * nt + i, 0, 0), memory_space=pltpu.SMEM)
    return pl.pallas_call(
        functools.partial(_norm_scatter_kernel, n_steps=b * nt),
        grid=(b, nt),
        in_specs=[smem_blk, smem_blk,
                  pl.BlockSpec((None, tm, d), lambda bi, i: (bi, i, 0)),
                  pl.BlockSpec((1, d), lambda bi, i: (0, 0)),
                  pl.BlockSpec(memory_space=pl.ANY)],
        out_specs=pl.BlockSpec(memory_space=pl.ANY),
        out_shape=jax.ShapeDtypeStruct((n_slots, d), F32),
        scratch_shapes=[pltpu.VMEM((2, tm, d), F32), pltpu.SemaphoreType.DMA((2,))],
        input_output_aliases={4: 0},
        compiler_params=_params(("arbitrary", "arbitrary")),
    )(idx(slot1), idx(slot2), h3, g.reshape(1, d), jnp.zeros((n_slots, d), F32))


def _combine_kernel(s1_ref, s2_ref, n1_ref, n2_ref, h_ref, g1_ref, g2_ref, fg_ref, y_hbm, o_ref,
                    a_ref, b_ref, sem, *, n_steps):
    t = pl.program_id(0) * pl.num_programs(1) + pl.program_id(1)
    cur = t % 2
    n = a_ref.shape[1]

    def gather(buf, i1_ref, i2_ref):
        def start(r4, carry):
            for k in range(ROW_DMA_UNROLL):
                r = r4 * ROW_DMA_UNROLL + k
                _row_copy(y_hbm, a_ref.at[buf], sem.at[buf], i1_ref[0, 0, r], r).start()
                _row_copy(y_hbm, b_ref.at[buf], sem.at[buf], i2_ref[0, 0, r], r).start()
            return carry

        lax.fori_loop(0, n // ROW_DMA_UNROLL, start, 0)

    @pl.when(t == 0)
    def _():
        gather(0, s1_ref, s2_ref)

    @pl.when(t + 1 < n_steps)
    def _():
        gather(1 - cur, n1_ref, n2_ref)

    def wait(r, carry):
        _row_copy(y_hbm, a_ref.at[cur], sem.at[cur], 0, r).wait()
        _row_copy(y_hbm, b_ref.at[cur], sem.at[cur], 0, r).wait()
        return carry

    lax.fori_loop(0, n, wait, 0)
    h = h_ref[...] + g1_ref[...] * a_ref[cur] + g2_ref[...] * b_ref[cur]
    ms = jnp.mean(h * h, axis=-1, keepdims=True)
    o_ref[...] = h * lax.rsqrt(ms + EPS) * fg_ref[...]


def _moe_combine_final(h3, y, slot1, slot2, g1, g2, final_g, n_real, tm=256):
    b, _, d = h3.shape
    nt = n_real // tm
    idx = lambda a: a.reshape(b * nt, 1, tm)
    gate = lambda a: a.reshape(b, n_real, 1)
    smem_blk = pl.BlockSpec((1, 1, tm), lambda bi, i: (bi * nt + i, 0, 0), memory_space=pltpu.SMEM)
    smem_next = pl.BlockSpec((1, 1, tm), lambda bi, i: (jnp.minimum(bi * nt + i + 1, b * nt - 1), 0, 0),
                             memory_space=pltpu.SMEM)
    return pl.pallas_call(
        functools.partial(_combine_kernel, n_steps=b * nt),
        grid=(b, nt),
        in_specs=[smem_blk, smem_blk, smem_next, smem_next,
                  pl.BlockSpec((None, tm, d), lambda bi, i: (bi, i, 0)),
                  pl.BlockSpec((None, tm, 1), lambda bi, i: (bi, i, 0)),
                  pl.BlockSpec((None, tm, 1), lambda bi, i: (bi, i, 0)),
                  pl.BlockSpec((1, d), lambda bi, i: (0, 0)),
                  pl.BlockSpec(memory_space=pl.ANY)],
        out_specs=pl.BlockSpec((None, tm, d), lambda bi, i: (bi, i, 0)),
        out_shape=jax.ShapeDtypeStruct((b, n_real, d), F32),
        scratch_shapes=[pltpu.VMEM((2, tm, d), F32), pltpu.VMEM((2, tm, d), F32),
                        pltpu.SemaphoreType.DMA((2,))],
        compiler_params=_params(("arbitrary", "arbitrary")),
    )(idx(slot1), idx(slot2), idx(slot1), idx(slot2), h3, gate(g1), gate(g2), final_g.reshape(1, d), y)


def _filter_features(l, lp, n_meta, n_emb):
    r = np.arange(lp)
    pos = (r + n_meta) % l
    valid = r < l
    t = pos / (l - 1.0)
    bands = (n_emb - 1) // 2
    w = 2.0 * np.pi * pos / l
    f = np.linspace(1e-4, bands - 1, bands)
    z = np.zeros((lp, V7X_LANES), np.float64)
    z[:, 0] = t
    z[:, 1:1 + bands] = np.cos(f[None, :] * w[:, None])
    z[:, 1 + bands:1 + 2 * bands] = -np.sin(f[None, :] * w[:, None])
    z *= valid[:, None]
    z[:, V7X_LANES - 1] = valid
    return jnp.asarray(z, F32)


def kernel(x, meta_tokens, norm_mix_g, w_in, hy_conv_w, hy_conv_b, hy_f_w1, hy_f_b1, hy_f_w2, hy_f_b2, hy_f_w3, hy_f_b3, hy_f_wo, hy_f_freq, hy_decay, hy_skip, cv_dw_w, cv_dw_b, cv_ln_g, cv_ln_b, attn_sink, group_norm_g, w_out, norm_ffn_g, ffn_w1, ffn_w3, ffn_w2, router_w, router_b, moe_w1, moe_w3, moe_w2, final_norm_g):
    b, n_real, d = x.shape
    n_meta = meta_tokens.shape[0]
    depth = w_in.shape[0]
    l = n_real + n_meta
    m = b * l
    lp = _round_up(l, 3 * V7X_LANES)
    w_hy = hy_skip.shape[1]
    w_fn = w_hy
    w_cv = cv_dw_b.shape[1]
    kv_w = N_KV_HEADS * HEAD_DIM
    p_in = w_in.shape[2]
    w_at = p_in - 3 * w_hy - w_fn - 2 * w_cv - 2 * kv_w
    n_q_heads = w_at // HEAD_DIM
    d_ff = ffn_w1.shape[2]
    assert depth == 2 and router_w.shape[2] == N_EXPERTS
    assert n_meta % V7X_BF16_ROWS == 0 and n_real % 256 == 0 and l % 16 == 0
    off_fn = 3 * w_hy
    off_cv = off_fn + w_fn
    off_q = off_cv + 2 * w_cv
    off_k = off_q + w_at
    off_v = off_k + kv_w

    r = np.arange(lp)
    pos = (r + n_meta) % l
    time_valid = r < l
    freq = np.arange(lp)
    cos_f, sin_f = _dft_tables(pos, time_valid, np.pi / l, 2 * l, lp // 16, 0, lp)
    cos_i, sin_i = _dft_tables(freq, freq <= l, np.pi / l, 2 * l, l // 16,
                               (l - n_meta) // 16, lp)
    cos_l, sin_l = _dft_tables(pos, time_valid, 2 * np.pi / l, l, l // 16,
                               (l - n_meta) // 16, lp)
    wf = np.where(freq <= l, np.where((freq == 0) | (freq == l), 1.0, 2.0), 0.0) / (2.0 * l)
    wf = jnp.asarray(wf.reshape(lp, 1), F32)
    zc = _filter_features(l, lp, n_meta, hy_f_w1.shape[1])

    h = jnp.concatenate([x, jnp.broadcast_to(meta_tokens[None].astype(x.dtype), (b, n_meta, d))], axis=1)
    h = h.reshape(m, d)
    out = None
    for layer in range(depth):
        xn = _rmsnorm(h, norm_mix_g[layer], BF16)
        proj = _matmul(xn, [w_in], layer, tk=d, k_blk=0, n_out=p_in, tn=512, tm=1040, out_dtype=F32)
        proj3 = proj.reshape(b, l, p_in)

        hs, hd = _hyena_filter_sums(zc, hy_f_w1[layer], hy_f_b1[layer], hy_f_w2[layer], hy_f_b2[layer],
                                    hy_f_w3[layer], hy_f_b3[layer], hy_f_wo[layer], hy_f_freq[layer],
                                    hy_decay[layer], w_hy)
        kre, kim = _hyena_filter_spectrum(cos_f, sin_f, hs, hd, wf)
        x0c, vx = _hyena_prep(proj3, hy_conv_w[layer], hy_conv_b[layer], w_hy, n_real, n_meta, lp)
        yre, yim = _hyena_fwd(cos_f, sin_f, vx, kre, kim)
        y_a = _hyena_inv(cos_i, sin_i, yre, yim, vx, x0c, hy_skip[layer])

        fa, fb = _fnet_chan(proj3, off_fn // w_fn, w_fn, lp)
        y_b = _fnet_time(cos_l, sin_l, fa, fb, l)

        y_c = _conformer_conv(proj3, off_cv // 128, cv_dw_w[layer], cv_dw_b[layer], w_cv, n_real, n_meta)
        y_d = _windowed_attention(proj3, attn_sink[layer], off_q, off_k, off_v, n_q_heads, n_real, n_meta)

        ycat = _group_norms(y_a.reshape(m, w_hy), y_b.reshape(m, w_fn), y_c.reshape(m, w_cv),
                            y_d.reshape(m, w_at), group_norm_g[layer], cv_ln_g[layer], cv_ln_b[layer])
        h = _matmul(ycat, [w_out], layer, tk=ycat.shape[1], k_blk=0, n_out=d, tn=512, tm=1040,
                    out_dtype=F32, residual=h)

        if layer % 2 == 0:
            i = layer // 2
            xn = _rmsnorm(h, norm_ffn_g[layer], BF16)
            tn_ff = _pick_tile(d_ff, (512, 256, 128))
            act = _matmul(xn, [ffn_w1, ffn_w3], i, tk=d, k_blk=0, n_out=d_ff, tn=tn_ff, tm=1040,
                          out_dtype=BF16, swiglu=True)
            n_k = 2 if d_ff % (2 * V7X_LANES) == 0 and d_ff > 4096 else 1
            for kb in range(n_k):
                h = _matmul(act, [ffn_w2], i, tk=d_ff // n_k, k_blk=kb, n_out=d, tn=512, tm=520,
                            out_dtype=F32, residual=h)
        else:
            i = layer // 2
            h3 = h.reshape(b, l, d)
            mr = b * n_real
            logits_t = _norm_router(h3, norm_ffn_g[layer], router_w[i], n_real)
            n_slots = _round_up(2 * mr + N_EXPERTS * (MOE_TILE - 1), MOE_TILE)
            n_tiles = n_slots // MOE_TILE
            slot1, slot2, g1, g2, te, tv, nx = _route(logits_t, router_b[i], n_tiles)
            slot1, slot2 = slot1.reshape(mr), slot2.reshape(mr)
            te, tv, nx = (a.reshape(-1)[:n_tiles] for a in (te, tv, nx))
            xs = _norm_scatter(h3, norm_ffn_g[layer], slot1, slot2, n_real, n_slots)
            act = _grouped_matmul(xs, [moe_w1[i], moe_w3[i]], te, tv, nx, tn=512, out_dtype=BF16, swiglu=True)
            ys = _grouped_matmul(act, [moe_w2[i]], te, tv, nx, tn=1024, out_dtype=F32)
            out = _moe_combine_final(h3, ys, slot1, slot2, g1, g2, final_norm_g, n_real)
    return out
```

```python
import functools
import math

import numpy as np
import jax
import jax.numpy as jnp
from jax import lax
from jax.experimental import pallas as pl
from jax.experimental.pallas import tpu as pltpu

F32 = jnp.float32
BF16 = jnp.bfloat16

V7X_LANES = 128
V7X_BF16_ROWS = 16
V7X_VMEM_LIMIT = 58 * 1024 * 1024

EPS = 1e-6
NEG = -1e30
HEAD_DIM = 128
N_KV_HEADS = 2
ATT_BLOCK = 128
FNET_GROUP_W = 256
N_EXPERTS = 8
MOE_TILE = 512
ROW_DMA_UNROLL = 4


def _round_up(x, m):
    return (x + m - 1) // m * m


def _pick_tile(n, candidates):
    for c in candidates:
        if n % c == 0:
            return c
    raise ValueError(f"no tile for {n} in {candidates}")


def _params(sem, vmem=None):
    if sem is None:
        return pltpu.CompilerParams(vmem_limit_bytes=vmem or V7X_VMEM_LIMIT)
    return pltpu.CompilerParams(dimension_semantics=sem, vmem_limit_bytes=vmem or V7X_VMEM_LIMIT)


def _rmsnorm_kernel(x_ref, g_ref, o_ref):
    x = x_ref[...]
    ms = jnp.mean(x * x, axis=-1, keepdims=True)
    o_ref[...] = (x * lax.rsqrt(ms + EPS) * g_ref[...]).astype(o_ref.dtype)


def _rmsnorm(x2d, g, out_dtype, tm=256):
    m, d = x2d.shape
    return pl.pallas_call(
        _rmsnorm_kernel,
        grid=(pl.cdiv(m, tm),),
        in_specs=[pl.BlockSpec((tm, d), lambda i: (i, 0)), pl.BlockSpec((1, d), lambda i: (0, 0))],
        out_specs=pl.BlockSpec((tm, d), lambda i: (i, 0)),
        out_shape=jax.ShapeDtypeStruct((m, d), out_dtype),
        compiler_params=_params(("arbitrary",)),
    )(x2d, g.reshape(1, d))


def _cast_tile(w_ref, wb_ref):
    k = w_ref.shape[0]
    ck = _pick_tile(k, (256, 128, 64, 32, 16))

    def body(c, carry):
        r = pl.multiple_of(c * ck, ck)
        wb_ref[pl.ds(r, ck), :] = w_ref[pl.ds(r, ck), :].astype(BF16)
        return carry

    lax.fori_loop(0, k // ck, body, 0)


def _mm_kernel(*refs, n_w, swiglu, has_res):
    x_ref = refs[0]
    w_refs = refs[1:1 + n_w]
    pos = 1 + n_w
    res_ref = refs[pos] if has_res else None
    pos += int(has_res)
    o_ref = refs[pos]
    wb_refs = refs[pos + 1:pos + 1 + n_w]

    @pl.when(pl.program_id(1) == 0)
    def _():
        for w_ref, wb_ref in zip(w_refs, wb_refs):
            _cast_tile(w_ref, wb_ref)

    x = x_ref[...]
    acc = jnp.dot(x, wb_refs[0][...], preferred_element_type=F32)
    if swiglu:
        up = jnp.dot(x, wb_refs[1][...], preferred_element_type=F32)
        acc = acc * jax.nn.sigmoid(acc) * up
    if has_res:
        acc = acc + res_ref[...]
    o_ref[...] = acc.astype(o_ref.dtype)


def _matmul(x, ws, layer, *, tk, k_blk, n_out, tn, tm, out_dtype, residual=None, swiglu=False):
    m = x.shape[0]
    n_w = len(ws)
    tm = min(tm, _round_up(m, V7X_BF16_ROWS))
    in_specs = [pl.BlockSpec((tm, tk), lambda j, i: (i, k_blk))]
    in_specs += [pl.BlockSpec((None, tk, tn), lambda j, i: (layer, k_blk, j)) for _ in ws]
    args = [x, *ws]
    if residual is not None:
        in_specs.append(pl.BlockSpec((tm, tn), lambda j, i: (i, j)))
        args.append(residual)
    return pl.pallas_call(
        functools.partial(_mm_kernel, n_w=n_w, swiglu=swiglu, has_res=residual is not None),
        grid=(n_out // tn, pl.cdiv(m, tm)),
        in_specs=in_specs,
        out_specs=pl.BlockSpec((tm, tn), lambda j, i: (i, j)),
        out_shape=jax.ShapeDtypeStruct((m, n_out), out_dtype),
        scratch_shapes=[pltpu.VMEM((tk, tn), BF16) for _ in ws],
        compiler_params=_params(("arbitrary", "arbitrary")),
    )(*args)


def _gmm_kernel(te_ref, tv_ref, nx_ref, x_ref, *refs, n_w, swiglu, tn):
    w_hbm = refs[:n_w]
    o_ref, stage_ref, wb_ref, sem = refs[n_w:n_w + 4]
    j = pl.program_id(0)
    t = pl.program_id(1)
    n_j = pl.num_programs(0)
    e = te_ref[t]

    def w_copy(i, expert, col_tile):
        col = pl.multiple_of(col_tile * tn, tn)
        return pltpu.make_async_copy(w_hbm[i].at[expert, :, pl.ds(col, tn)], stage_ref.at[i], sem.at[i])

    @pl.when((j == 0) & (t == 0))
    def _():
        for i in range(n_w):
            w_copy(i, e, 0).start()

    @pl.when((t == 0) | (e != te_ref[jnp.maximum(t - 1, 0)]))
    def _():
        for i in range(n_w):
            w_copy(i, e, j).wait()
            _cast_tile(stage_ref.at[i], wb_ref.at[i])
        e_next = nx_ref[t]

        @pl.when(e_next < N_EXPERTS)
        def _():
            for i in range(n_w):
                w_copy(i, e_next, j).start()

        @pl.when((e_next >= N_EXPERTS) & (j + 1 < n_j))
        def _():
            for i in range(n_w):
                w_copy(i, te_ref[0], j + 1).start()

    @pl.when(tv_ref[t] != 0)
    def _():
        x = x_ref[...].astype(BF16)
        acc = jnp.dot(x, wb_ref[0], preferred_element_type=F32)
        if swiglu:
            acc = acc * jax.nn.sigmoid(acc) * jnp.dot(x, wb_ref[1], preferred_element_type=F32)
        o_ref[...] = acc.astype(o_ref.dtype)

    @pl.when(tv_ref[t] == 0)
    def _():
        o_ref[...] = jnp.zeros_like(o_ref)


def _grouped_matmul(x, ws, tile_expert, tile_valid, next_expert, *, tn, out_dtype, swiglu=False):
    s, k = x.shape
    n = ws[0].shape[2]
    n_w = len(ws)
    grid_spec = pltpu.PrefetchScalarGridSpec(
        num_scalar_prefetch=3,
        grid=(n // tn, s // MOE_TILE),
        in_specs=[pl.BlockSpec((MOE_TILE, k), lambda j, t, te, tv, nx: (t, 0))]
        + [pl.BlockSpec(memory_space=pl.ANY) for _ in ws],
        out_specs=pl.BlockSpec((MOE_TILE, tn), lambda j, t, te, tv, nx: (t, j)),
        scratch_shapes=[pltpu.VMEM((n_w, k, tn), F32), pltpu.VMEM((n_w, k, tn), BF16),
                        pltpu.SemaphoreType.DMA((n_w,))],
    )
    return pl.pallas_call(
        functools.partial(_gmm_kernel, n_w=n_w, swiglu=swiglu, tn=tn),
        grid_spec=grid_spec,
        out_shape=jax.ShapeDtypeStruct((s, n), out_dtype),
        compiler_params=_params(("arbitrary", "arbitrary")),
    )(tile_expert, tile_valid, next_expert, x, *ws)


def _dft_table_kernel(ac_ref, as_ref, bc_ref, bs_ref, cc_ref, cs_ref, tc_ref, ts_ref, c_ref, s_ref, *,
                      n_blocks, n_rows_out):
    ac = ac_ref[...]
    asn = as_ref[...]

    def body(q, carry):
        v = q % 16
        u = q // 16
        bc, bs = bc_ref[pl.ds(v, 1), :], bs_ref[pl.ds(v, 1), :]
        cc, cs = cc_ref[pl.ds(u, 1), :], cs_ref[pl.ds(u, 1), :]
        dc = bc * cc - bs * cs
        dsn = bs * cc + bc * cs
        row = pl.multiple_of(q * 16, 16)
        c_ref[pl.ds(row, 16), :] = (ac * dc - asn * dsn).astype(BF16)
        s_ref[pl.ds(row, 16), :] = (asn * dc + ac * dsn).astype(BF16)
        return carry

    lax.fori_loop(0, n_blocks, body, 0)
    c_ref[pl.ds(16 * n_blocks, 16), :] = tc_ref[...].astype(BF16)
    s_ref[pl.ds(16 * n_blocks, 16), :] = ts_ref[...].astype(BF16)
    pad = n_rows_out - 16 * (n_blocks + 1)
    if pad:
        c_ref[pl.ds(16 * (n_blocks + 1), pad), :] = jnp.zeros((pad, c_ref.shape[1]), BF16)
        s_ref[pl.ds(16 * (n_blocks + 1), pad), :] = jnp.zeros((pad, s_ref.shape[1]), BF16)


def _dft_tables(col_pos, col_valid, unit, period, row0, row_step, n_blocks, tail_pos, n_rows_out):
    ncol = col_pos.shape[0]
    n_u = -(-n_blocks // 16)
    p = col_pos.astype(np.int64)[None, :]
    valid = col_valid[None, :].astype(np.float64)
    assert len(tail_pos) <= 16 and n_rows_out >= 16 * (n_blocks + 1)

    def factor(rowpos, mask):
        ang = unit * ((np.asarray(rowpos, dtype=np.int64)[:, None] * p) % period)
        return (jnp.asarray((np.cos(ang) * mask).astype(np.float32)),
                jnp.asarray((np.sin(ang) * mask).astype(np.float32)))

    a_c, a_s = factor(row0 + row_step * np.arange(16), valid)
    b_c, b_s = factor(row_step * 16 * np.arange(16), 1.0)
    c_c, c_s = factor(row_step * 256 * np.arange(n_u), 1.0)
    tail = np.zeros(16, np.int64)
    tail[:len(tail_pos)] = tail_pos
    t_c, t_s = factor(tail, valid * (np.arange(16) < len(tail_pos))[:, None])
    pc = _pick_tile(ncol, (384, 256, 128))
    kern = functools.partial(_dft_table_kernel, n_blocks=n_blocks, n_rows_out=n_rows_out)
    col = lambda rows: pl.BlockSpec((rows, pc), lambda j: (0, j))
    return pl.pallas_call(
        kern,
        grid=(ncol // pc,),
        in_specs=[col(16), col(16), col(16), col(16), col(n_u), col(n_u), col(16), col(16)],
        out_specs=[col(n_rows_out), col(n_rows_out)],
        out_shape=[jax.ShapeDtypeStruct((n_rows_out, ncol), BF16)] * 2,
        compiler_params=_params(("arbitrary",)),
    )(a_c, a_s, b_c, b_s, c_c, c_s, t_c, t_s)


def _filter_kernel(z_ref, w1_ref, b1_ref, w2_ref, b2_ref, w3_ref, b3_ref, wo_ref, fr_ref, dec_ref,
                   hs_ref, hd_ref, *, width):
    hp = lax.Precision.HIGHEST
    z = z_ref[...]
    fr = fr_ref[...]
    hid = jnp.sin(fr * (jnp.dot(z, w1_ref[...], precision=hp, preferred_element_type=F32) + b1_ref[...]))
    hid = jnp.sin(fr * (jnp.dot(hid, w2_ref[...], precision=hp, preferred_element_type=F32) + b2_ref[...]))
    hid = jnp.sin(fr * (jnp.dot(hid, w3_ref[...], precision=hp, preferred_element_type=F32) + b3_ref[...]))
    h = jnp.dot(hid, wo_ref[...], precision=hp, preferred_element_type=F32)
    t = z[:, 0:1]
    valid = z[:, V7X_LANES - 1:V7X_LANES]
    h = h * jnp.exp(-t * dec_ref[...]) * valid
    hf = h[:, :width]
    hb = h[:, width:]
    hs_ref[...] = (hf + hb).astype(BF16)
    hd_ref[...] = (hf - hb).astype(BF16)


def _hyena_filter_sums(zc, w1, b1, w2, b2, w3, b3, wo, freq, decay, width):
    lp = zc.shape[0]
    hid = w2.shape[0]
    tr = _pick_tile(lp, (528, 384, 256, 128))
    w1p = jnp.zeros((V7X_LANES, hid), F32).at[:w1.shape[0]].set(w1)
    full = lambda shape: pl.BlockSpec(shape, lambda i: (0, 0))
    return pl.pallas_call(
        functools.partial(_filter_kernel, width=width),
        grid=(lp // tr,),
        in_specs=[pl.BlockSpec((tr, V7X_LANES), lambda i: (i, 0)),
                  full((V7X_LANES, hid)), full((1, hid)), full((hid, hid)), full((1, hid)),
                  full((hid, hid)), full((1, hid)), full((hid, 2 * width)), full((1, hid)),
                  full((1, 2 * width))],
        out_specs=[pl.BlockSpec((tr, width), lambda i: (i, 0))] * 2,
        out_shape=[jax.ShapeDtypeStruct((lp, width), BF16)] * 2,
        compiler_params=_params(("arbitrary",)),
    )(zc, w1p, b1.reshape(1, hid), w2, b2.reshape(1, hid), w3, b3.reshape(1, hid), wo,
      freq.reshape(1, hid), decay.reshape(1, 2 * width))


def _kf_kernel(ce_ref, se_ref, co_ref, so_ref, hse_ref, hso_ref, hde_ref, hdo_ref, w1_ref, w2_ref,
               k1re_ref, k1im_ref, k2re_ref, k2im_ref):
    ka = jnp.dot(ce_ref[...], hse_ref[...], preferred_element_type=F32)
    kb = jnp.dot(co_ref[...], hso_ref[...], preferred_element_type=F32)
    sa = jnp.dot(se_ref[...], hde_ref[...], preferred_element_type=F32)
    sb = jnp.dot(so_ref[...], hdo_ref[...], preferred_element_type=F32)
    w1 = w1_ref[...]
    w2 = w2_ref[...]
    k1re_ref[...] = w1 * (ka + kb)
    k1im_ref[...] = -w1 * (sa + sb)
    k2re_ref[...] = w2 * (ka - kb)
    k2im_ref[...] = w2 * (sa - sb)


def _half_tile(n):
    return _pick_tile(n, (544, 384, 256, 128))


def _hyena_filter_spectrum(tabs_e, tabs_o, hs, hd, w1, w2):
    (ce, se), (co, so) = tabs_e, tabs_o
    f2p, lhp = ce.shape
    width = hs.shape[1]
    tf = _half_tile(f2p)
    cb = 512
    tab = pl.BlockSpec((tf, lhp), lambda c, f: (f, 0))
    half = lambda which: pl.BlockSpec((lhp, cb), lambda c, f: (which, c))
    wcol = pl.BlockSpec((tf, 1), lambda c, f: (f, 0))
    return pl.pallas_call(
        _kf_kernel,
        grid=(width // cb, f2p // tf),
        in_specs=[tab, tab, tab, tab, half(0), half(1), half(0), half(1), wcol, wcol],
        out_specs=[pl.BlockSpec((tf, cb), lambda c, f: (f, c))] * 4,
        out_shape=[jax.ShapeDtypeStruct((f2p, width), F32)] * 4,
        compiler_params=_params(("arbitrary", "arbitrary")),
    )(ce, se, co, so, hs, hs, hd, hd, w1, w2)


def _hyena_prep_kernel(x0_ref, x1_ref, v_ref, w0_ref, w1_ref, wv_ref, b0_ref, b1_ref, bv_ref,
                       x0c_ref, vxe_ref, vxo_ref, pad_ref, vx_ref, *, n_real, n_meta):
    cb = pad_ref.shape[1]
    l = n_real + n_meta
    pad_ref[pl.ds(0, 8), :] = jnp.zeros((8, cb), F32)
    pad_ref[pl.ds(8 + l, 8), :] = jnp.zeros((8, cb), F32)

    def conv(u_ref, w_ref, b_ref):
        pad_ref[pl.ds(8, n_meta), :] = u_ref[pl.ds(n_real, n_meta), :]
        pad_ref[pl.ds(8 + n_meta, n_real), :] = u_ref[pl.ds(0, n_real), :]
        w = w_ref[...]
        b = b_ref[...]

        def at(start, n):
            return (w[0:1] * pad_ref[pl.ds(start - 1, n), :] + w[1:2] * pad_ref[pl.ds(start, n), :]
                    + w[2:3] * pad_ref[pl.ds(start + 1, n), :] + b)

        return at(8 + n_meta, n_real), at(8, n_meta)

    x0_real, x0_meta = conv(x0_ref, w0_ref, b0_ref)
    x0c_ref[pl.ds(0, n_real), :] = x0_real
    x0c_ref[pl.ds(n_real, n_meta), :] = x0_meta
    x1_real, x1_meta = conv(x1_ref, w1_ref, b1_ref)
    v_real, v_meta = conv(v_ref, wv_ref, bv_ref)
    vx_ref[pl.ds(0, n_real), :] = v_real * x1_real
    vx_ref[pl.ds(n_real, n_meta), :] = v_meta * x1_meta
    lh = l // 2
    lhp = vxe_ref.shape[0]
    for par, o_ref in ((0, vxe_ref), (1, vxo_ref)):
        o_ref[pl.ds(0, lh), :] = vx_ref[pl.ds(par, lh, stride=2), :].astype(BF16)
        o_ref[pl.ds(lh, lhp - lh), :] = jnp.zeros((lhp - lh, cb), BF16)


def _hyena_prep(proj3, conv_w, conv_b, width, n_real, n_meta, lhp):
    b, l, _ = proj3.shape
    cb = 128
    nc = width // cb
    col = lambda off: pl.BlockSpec((None, l, cb), lambda bi, c: (bi, 0, off + c))
    wcol = lambda off: pl.BlockSpec((3, cb), lambda bi, c: (0, off + c))
    bcol = lambda off: pl.BlockSpec((1, cb), lambda bi, c: (0, off + c))
    cbias = conv_b.reshape(1, 3 * width)
    half = pl.BlockSpec((None, lhp, cb), lambda bi, c: (bi, 0, c))
    return pl.pallas_call(
        functools.partial(_hyena_prep_kernel, n_real=n_real, n_meta=n_meta),
        grid=(b, nc),
        in_specs=[col(0), col(nc), col(2 * nc), wcol(0), wcol(nc), wcol(2 * nc),
                  bcol(0), bcol(nc), bcol(2 * nc)],
        out_specs=[pl.BlockSpec((None, l, cb), lambda bi, c: (bi, 0, c)), half, half],
        out_shape=[jax.ShapeDtypeStruct((b, l, width), F32)]
        + [jax.ShapeDtypeStruct((b, lhp, width), BF16)] * 2,
        scratch_shapes=[pltpu.VMEM((l + 16, cb), F32), pltpu.VMEM((l, cb), F32)],
        compiler_params=_params(("arbitrary", "arbitrary")),
    )(proj3, proj3, proj3, conv_w, conv_w, conv_w, cbias, cbias, cbias)


def _hyena_fwd_kernel(ce_ref, se_ref, co_ref, so_ref, vxe_ref, vxo_ref, k1re_ref, k1im_ref, k2re_ref, k2im_ref,
                      ere_ref, eim_ref, ore_ref, oim_ref):
    vxe = vxe_ref[...]
    vxo = vxo_ref[...]
    a_re = jnp.dot(ce_ref[...], vxe, preferred_element_type=F32)
    b_re = jnp.dot(co_ref[...], vxo, preferred_element_type=F32)
    a_im = jnp.dot(se_ref[...], vxe, preferred_element_type=F32)
    b_im = jnp.dot(so_ref[...], vxo, preferred_element_type=F32)
    u1re, u1im = a_re + b_re, a_im + b_im
    u2re, u2im = a_re - b_re, b_im - a_im
    k1re, k1im, k2re, k2im = k1re_ref[...], k1im_ref[...], k2re_ref[...], k2im_ref[...]
    p_re = u1re * k1re + u1im * k1im
    p_im = u1re * k1im - u1im * k1re
    q_re = u2re * k2re + u2im * k2im
    q_im = u2re * k2im - u2im * k2re
    ere_ref[...] = (p_re + q_re).astype(BF16)
    eim_ref[...] = (p_im - q_im).astype(BF16)
    ore_ref[...] = (p_re - q_re).astype(BF16)
    oim_ref[...] = (p_im + q_im).astype(BF16)


def _hyena_fwd(tabs_e, tabs_o, vxe, vxo, kspec):
    (ce, se), (co, so) = tabs_e, tabs_o
    f2p, lhp = ce.shape
    b, _, width = vxe.shape
    tf = _half_tile(f2p)
    cb = 512
    tab = pl.BlockSpec((tf, lhp), lambda c, bi, f: (f, 0))
    vx_blk = pl.BlockSpec((None, lhp, cb), lambda c, bi, f: (bi, 0, c))
    k_blk = pl.BlockSpec((tf, cb), lambda c, bi, f: (f, c))
    return pl.pallas_call(
        _hyena_fwd_kernel,
        grid=(width // cb, b, f2p // tf),
        in_specs=[tab, tab, tab, tab, vx_blk, vx_blk, k_blk, k_blk, k_blk, k_blk],
        out_specs=[pl.BlockSpec((None, tf, cb), lambda c, bi, f: (bi, f, c))] * 4,
        out_shape=[jax.ShapeDtypeStruct((b, f2p, width), BF16)] * 4,
        compiler_params=_params(("arbitrary", "arbitrary", "arbitrary")),
    )(ce, se, co, so, vxe, vxo, *kspec)


def _hyena_inv_kernel(cet_ref, set_ref, cot_ref, sot_ref, ere_ref, eim_ref, ore_ref, oim_ref,
                      vxe_ref, vxo_ref, x0c_ref, skip_ref, o_ref, mix_ref):
    tr = vxe_ref.shape[0]
    skip = skip_ref[...]
    ye = (jnp.dot(cet_ref[...], ere_ref[...], preferred_element_type=F32)
          - jnp.dot(set_ref[...], eim_ref[...], preferred_element_type=F32))
    yo = (jnp.dot(cot_ref[...], ore_ref[...], preferred_element_type=F32)
          - jnp.dot(sot_ref[...], oim_ref[...], preferred_element_type=F32))
    te = ye + skip * vxe_ref[...].astype(F32)
    to = yo + skip * vxo_ref[...].astype(F32)
    for c in range(o_ref.shape[1] // V7X_LANES):
        lanes = slice(c * V7X_LANES, (c + 1) * V7X_LANES)
        mix_ref[c, pl.ds(0, tr, stride=2), :] = te[:, lanes]
        mix_ref[c, pl.ds(1, tr, stride=2), :] = to[:, lanes]
        o_ref[:, lanes] = mix_ref[c] * x0c_ref[:, lanes]


def _hyena_inv(tabs_et, tabs_ot, spec, vxe, vxo, x0c, skip):
    (cet, set_), (cot, sot) = tabs_et, tabs_ot
    lhp, f2p = cet.shape
    b, l, width = x0c.shape
    tr = _half_tile(lhp)
    cb = 512
    tab = pl.BlockSpec((tr, f2p), lambda c, bi, r: (r, 0))
    s_blk = pl.BlockSpec((None, f2p, cb), lambda c, bi, r: (bi, 0, c))
    vx_blk = pl.BlockSpec((None, tr, cb), lambda c, bi, r: (bi, r, c))
    row_blk = pl.BlockSpec((None, 2 * tr, cb), lambda c, bi, r: (bi, r, c))
    return pl.pallas_call(
        _hyena_inv_kernel,
        grid=(width // cb, b, lhp // tr),
        in_specs=[tab, tab, tab, tab, s_blk, s_blk, s_blk, s_blk, vx_blk, vx_blk, row_blk,
                  pl.BlockSpec((1, cb), lambda c, bi, r: (0, c))],
        out_specs=row_blk,
        out_shape=jax.ShapeDtypeStruct((b, l, width), F32),
        scratch_shapes=[pltpu.VMEM((cb // V7X_LANES, 2 * tr, V7X_LANES), F32)],
        compiler_params=_params(("arbitrary", "arbitrary", "arbitrary")),
    )(cet, set_, cot, sot, *spec, vxe, vxo, x0c, skip.reshape(1, width))


def _fnet_chan_kernel(u_ref, cc_ref, sc_ref, a_ref, b_ref, *, l, tr):
    row = pl.program_id(1) * tr + lax.broadcasted_iota(jnp.int32, (tr, 1), 0)
    u = jnp.where(row < l, u_ref[...], 0.0).astype(BF16)
    gw = FNET_GROUP_W
    for g in range(u.shape[1] // gw):
        ug = u[:, g * gw:(g + 1) * gw]
        a_ref[:, g * gw:(g + 1) * gw] = jnp.dot(ug, cc_ref[...], preferred_element_type=F32).astype(BF16)
        b_ref[:, g * gw:(g + 1) * gw] = jnp.dot(ug, sc_ref[...], preferred_element_type=F32).astype(BF16)


def _fnet_chan(proj3, col_blk, width, lp):
    b, l, _ = proj3.shape
    tr = _pick_tile(lp, (384, 256, 128))
    k = np.arange(FNET_GROUP_W)
    ang = 2.0 * np.pi * ((k[:, None] * k[None, :]) % FNET_GROUP_W) / FNET_GROUP_W
    cc = jnp.asarray(np.cos(ang), BF16)
    sc = jnp.asarray(np.sin(ang), BF16)
    return pl.pallas_call(
        functools.partial(_fnet_chan_kernel, l=l, tr=tr),
        grid=(b, lp // tr),
        in_specs=[pl.BlockSpec((None, tr, width), lambda bi, r: (bi, r, col_blk)),
                  pl.BlockSpec((FNET_GROUP_W, FNET_GROUP_W), lambda bi, r: (0, 0)),
                  pl.BlockSpec((FNET_GROUP_W, FNET_GROUP_W), lambda bi, r: (0, 0))],
        out_specs=[pl.BlockSpec((None, tr, width), lambda bi, r: (bi, r, 0))] * 2,
        out_shape=[jax.ShapeDtypeStruct((b, lp, width), BF16)] * 2,
        compiler_params=_params(("arbitrary", "arbitrary")),
    )(proj3, cc, sc)


def _fnet_time_kernel(c_ref, s_ref, a_ref, b_ref, o_ref, *, scale):
    y = (jnp.dot(c_ref[...], a_ref[...], preferred_element_type=F32)
         - jnp.dot(s_ref[...], b_ref[...], preferred_element_type=F32))
    o_ref[...] = y * scale


def _fnet_time(cos_l, sin_l, a, bm, l):
    lp = cos_l.shape[0]
    b, _, width = a.shape
    tr = _pick_tile(lp, (384, 256, 128))
    cb = 512
    scale = 1.0 / math.sqrt(l * FNET_GROUP_W)
    return pl.pallas_call(
        functools.partial(_fnet_time_kernel, scale=scale),
        grid=(width // cb, b, lp // tr),
        in_specs=[pl.BlockSpec((tr, lp), lambda c, bi, r: (r, 0)),
                  pl.BlockSpec((tr, lp), lambda c, bi, r: (r, 0)),
                  pl.BlockSpec((None, lp, cb), lambda c, bi, r: (bi, 0, c)),
                  pl.BlockSpec((None, lp, cb), lambda c, bi, r: (bi, 0, c))],
        out_specs=pl.BlockSpec((None, tr, cb), lambda c, bi, r: (bi, r, c)),
        out_shape=jax.ShapeDtypeStruct((b, l, width), F32),
        compiler_params=_params(("arbitrary", "arbitrary", "arbitrary")),
    )(cos_l, sin_l, a, bm)


def _conformer_kernel(a_ref, g_ref, w_ref, b_ref, o_ref, pad_ref, *, n_real, n_meta, taps, chunk):
    cb = pad_ref.shape[1]
    half = taps // 2
    lead = _round_up(half, 8)
    l = n_real + n_meta
    pad_ref[pl.ds(0, lead), :] = jnp.zeros((lead, cb), F32)
    pad_ref[pl.ds(lead + l, lead), :] = jnp.zeros((lead, cb), F32)
    pad_ref[pl.ds(lead, n_meta), :] = (a_ref[pl.ds(n_real, n_meta), :]
                                       * jax.nn.sigmoid(g_ref[pl.ds(n_real, n_meta), :]))
    pad_ref[pl.ds(lead + n_meta, n_real), :] = (a_ref[pl.ds(0, n_real), :]
                                                * jax.nn.sigmoid(g_ref[pl.ds(0, n_real), :]))
    w = w_ref[...]
    bias = b_ref[...]

    def window(start, n):
        acc = jnp.broadcast_to(bias, (n, cb))
        for j in range(taps):
            acc = acc + w[j:j + 1] * pad_ref[pl.ds(start + j - half, n), :]
        return acc

    def body(ci, carry):
        base = pl.multiple_of(ci * chunk, chunk)
        o_ref[pl.ds(base, chunk), :] = window(base + lead + n_meta, chunk)
        return carry

    lax.fori_loop(0, n_real // chunk, body, 0)
    o_ref[pl.ds(n_real, n_meta), :] = window(lead, n_meta)


def _conformer_conv(proj3, col_blk, dw_w, dw_b, width, n_real, n_meta):
    b, l, _ = proj3.shape
    taps = dw_w.shape[0]
    cb = 128
    nc = width // cb
    lead = _round_up(taps // 2, 8)
    return pl.pallas_call(
        functools.partial(_conformer_kernel, n_real=n_real, n_meta=n_meta, taps=taps, chunk=64),
        grid=(b, nc),
        in_specs=[pl.BlockSpec((None, l, cb), lambda bi, c: (bi, 0, col_blk + c)),
                  pl.BlockSpec((None, l, cb), lambda bi, c: (bi, 0, col_blk + nc + c)),
                  pl.BlockSpec((taps, cb), lambda bi, c: (0, c)),
                  pl.BlockSpec((1, cb), lambda bi, c: (0, c))],
        out_specs=pl.BlockSpec((None, l, cb), lambda bi, c: (bi, 0, c)),
        out_shape=jax.ShapeDtypeStruct((b, l, width), F32),
        scratch_shapes=[pltpu.VMEM((l + 2 * lead, cb), F32)],
        compiler_params=_params(("arbitrary", "arbitrary")),
    )(proj3, proj3, dw_w, dw_b.reshape(1, width))


def _softmax_pv(parts, sink, out_dtype=F32):
    def rowwise(reduce_fn, combine, arrays):
        groups = {}
        for a in arrays:
            groups.setdefault(a.shape, []).append(a)
        return [reduce_fn(functools.reduce(combine, g), axis=-1, keepdims=True) for g in groups.values()]

    m = functools.reduce(jnp.maximum, rowwise(jnp.max, jnp.maximum, [s for s, _ in parts]), sink)
    es = [jnp.exp(s - m) for s, _ in parts]
    denom = functools.reduce(jnp.add, rowwise(jnp.sum, jnp.add, es), jnp.exp(sink - m))
    acc = None
    for e, (_, v) in zip(es, parts):
        pv = jnp.dot(e.astype(BF16), v, preferred_element_type=F32)
        acc = pv if acc is None else acc + pv
    return (acc / denom).astype(out_dtype)


def _qk(q, k):
    return lax.dot_general(q, k, (((1,), (1,)), ((), ())), preferred_element_type=F32)


def _attn_kernel(slope_ref, sink_ref, q_ref, kp_ref, kc_ref, kn_ref, km_ref, vp_ref, vc_ref, vn_ref, vm_ref,
                 o_ref, *, nb, n_meta, q_per_kv):
    g = pl.program_id(1)
    i = pl.program_id(2)
    t = ATT_BLOCK
    rows = q_per_kv * t
    row = lax.broadcasted_iota(jnp.int32, (rows, t), 0)
    a = row & (t - 1)
    c = lax.broadcasted_iota(jnp.int32, (rows, t), 1)
    hrow = lax.broadcasted_iota(jnp.int32, (rows, 1), 0) // t
    slope = jnp.zeros((rows, 1), F32)
    sink = jnp.zeros((rows, 1), F32)
    for hh in range(q_per_kv):
        slope = jnp.where(hrow == hh, slope_ref[g * q_per_kv + hh], slope)
        sink = jnp.where(hrow == hh, sink_ref[g * q_per_kv + hh], sink)
    scale = 1.0 / math.sqrt(HEAD_DIM)
    q = jnp.concatenate([q_ref[:, hh * HEAD_DIM:(hh + 1) * HEAD_DIM] for hh in range(q_per_kv)], axis=0)
    q = (q * scale).astype(BF16)
    km = km_ref[...].astype(BF16)
    vm = vm_ref[...].astype(BF16)
    kn = kn_ref[...].astype(BF16)
    vn = vn_ref[...].astype(BF16)

    def store(o):
        o_ref[...] = jnp.concatenate([o[hh * t:(hh + 1) * t] for hh in range(q_per_kv)], axis=1)

    @pl.when(i < nb)
    def _():
        d_prev = (t + a - c).astype(F32)
        d_cur = jnp.abs(a - c).astype(F32)
        d_next = (t + c - a).astype(F32)
        ok_prev = c >= a + jnp.where(i > 0, 0, t)
        ok_next = c <= a - jnp.where(i < nb - 1, 0, t)
        kp, kc = kp_ref[...].astype(BF16), kc_ref[...].astype(BF16)
        vp, vc = vp_ref[...].astype(BF16), vc_ref[...].astype(BF16)
        s_m = _qk(q, km)
        s_p = jnp.where(ok_prev, _qk(q, kp) - slope * d_prev, NEG)
        s_c = _qk(q, kc) - slope * d_cur
        s_n = jnp.where(ok_next, _qk(q, kn) - slope * d_next, NEG)
        store(_softmax_pv([(s_m, vm), (s_p, vp), (s_c, vc), (s_n, vn)], sink))

    @pl.when(i == nb)
    def _():
        delta = n_meta + c - a
        s_m = _qk(q, km)
        s_0 = jnp.where(delta <= t, _qk(q, kn) - slope * delta.astype(F32), NEG)
        store(_softmax_pv([(s_m, vm), (s_0, vn)], sink))


def _windowed_attention(proj3, sink, q_col, k_col, v_col, n_q_heads, n_real, n_meta):
    b, l, _ = proj3.shape
    nb = n_real // ATT_BLOCK
    q_per_kv = n_q_heads // N_KV_HEADS
    qw = q_per_kv * HEAD_DIM
    qb, kb, vb = q_col // qw, k_col // HEAD_DIM, v_col // HEAD_DIM
    mb = n_real // n_meta
    slopes = jnp.asarray([2.0 ** (-8.0 * (h + 1) / n_q_heads) for h in range(n_q_heads)], F32)
    smem = pl.BlockSpec(memory_space=pltpu.SMEM)

    def kv_blk(colb, shift):
        def index(bi, g, i):
            blk = jnp.clip(i + shift, 0, nb - 1)
            if shift == 1:
                blk = jnp.where(i == nb, 0, blk)
            return bi, blk, colb + g
        return pl.BlockSpec((None, ATT_BLOCK, HEAD_DIM), index)

    kv_meta = lambda colb: pl.BlockSpec((None, n_meta, HEAD_DIM), lambda bi, g, i: (bi, mb, colb + g))
    return pl.pallas_call(
        functools.partial(_attn_kernel, nb=nb, n_meta=n_meta, q_per_kv=q_per_kv),
        grid=(b, N_KV_HEADS, nb + 1),
        in_specs=[smem, smem,
                  pl.BlockSpec((None, ATT_BLOCK, qw), lambda bi, g, i: (bi, i, qb + g)),
                  kv_blk(kb, -1), kv_blk(kb, 0), kv_blk(kb, 1), kv_meta(kb),
                  kv_blk(vb, -1), kv_blk(vb, 0), kv_blk(vb, 1), kv_meta(vb)],
        out_specs=pl.BlockSpec((None, ATT_BLOCK, qw), lambda bi, g, i: (bi, i, g)),
        out_shape=jax.ShapeDtypeStruct((b, l, n_q_heads * HEAD_DIM), F32),
        compiler_params=_params(("arbitrary", "arbitrary", "arbitrary")),
    )(slopes, sink, proj3, proj3, proj3, proj3, proj3, proj3, proj3, proj3, proj3)


def _group_norm_kernel(ya_ref, yb_ref, yc_ref, yd_ref, g_ref, lng_ref, lnb_ref, o_ref, *, widths):
    def rms(y, g):
        return y * lax.rsqrt(jnp.mean(y * y, axis=-1, keepdims=True) + EPS) * g

    off = 0
    for idx, (y_ref, w) in enumerate(zip((ya_ref, yb_ref, yc_ref, yd_ref), widths)):
        y = y_ref[...]
        if idx == 2:
            mu = jnp.mean(y, axis=-1, keepdims=True)
            yc = y - mu
            var = jnp.mean(yc * yc, axis=-1, keepdims=True)
            y = yc * lax.rsqrt(var + EPS) * lng_ref[...] + lnb_ref[...]
            y = y * jax.nn.sigmoid(y)
        o_ref[:, off:off + w] = rms(y, g_ref[:, off:off + w]).astype(o_ref.dtype)
        off += w


def _group_norms(ya, yb, yc, yd, group_g, ln_g, ln_b, tm=256):
    m = ya.shape[0]
    widths = (ya.shape[1], yb.shape[1], yc.shape[1], yd.shape[1])
    tot = sum(widths)
    row = lambda w: pl.BlockSpec((tm, w), lambda i: (i, 0))
    full = lambda w: pl.BlockSpec((1, w), lambda i: (0, 0))
    return pl.pallas_call(
        functools.partial(_group_norm_kernel, widths=widths),
        grid=(pl.cdiv(m, tm),),
        in_specs=[row(widths[0]), row(widths[1]), row(widths[2]), row(widths[3]),
                  full(tot), full(widths[2]), full(widths[2])],
        out_specs=row(tot),
        out_shape=jax.ShapeDtypeStruct((m, tot), BF16),
        compiler_params=_params(("arbitrary",)),
    )(ya, yb, yc, yd, group_g.reshape(1, tot), ln_g.reshape(1, -1), ln_b.reshape(1, -1))


def _norm_router_kernel(h_ref, g_ref, w_ref, o_ref):
    x = h_ref[...]
    ms = jnp.mean(x * x, axis=-1, keepdims=True)
    xn = (x * lax.rsqrt(ms + EPS) * g_ref[...]).astype(BF16)
    o_ref[...] = lax.dot_general(w_ref[...], xn, (((1,), (1,)), ((), ())), preferred_element_type=F32)


def _norm_router(h3, g, router_w, n_real, tm=256):
    b, _, d = h3.shape
    nt = n_real // tm
    wt = router_w.T.astype(BF16)
    return pl.pallas_call(
        _norm_router_kernel,
        grid=(b, nt),
        in_specs=[pl.BlockSpec((None, tm, d), lambda bi, i: (bi, i, 0)),
                  pl.BlockSpec((1, d), lambda bi, i: (0, 0)),
                  pl.BlockSpec((N_EXPERTS, d), lambda bi, i: (0, 0))],
        out_specs=pl.BlockSpec((N_EXPERTS, tm), lambda bi, i: (0, bi * nt + i)),
        out_shape=jax.ShapeDtypeStruct((N_EXPERTS, b * n_real), F32),
        compiler_params=_params(("arbitrary", "arbitrary")),
    )(h3, g.reshape(1, d), wt)


def _route_kernel(lg_ref, b_ref, slot1_ref, slot2_ref, g1_ref, g2_ref, te_ref, tv_ref, nx_ref, *, chunk):
    ne, m = lg_ref.shape
    logits = lg_ref[...] + b_ref[...]
    row = lax.broadcasted_iota(jnp.int32, (ne, m), 0)
    m1 = jnp.max(logits, axis=0, keepdims=True)
    e1 = jnp.min(jnp.where(logits == m1, row, ne), axis=0, keepdims=True)
    rest = jnp.where(row == e1, -jnp.inf, logits)
    m2 = jnp.max(rest, axis=0, keepdims=True)
    e2 = jnp.min(jnp.where(rest == m2, row, ne), axis=0, keepdims=True)
    g1 = 1.0 / (1.0 + jnp.exp(m2 - m1))
    g1_ref[...] = g1
    g2_ref[...] = 1.0 - g1
    oh1 = (row == e1).astype(F32)
    oh2 = (row == e2).astype(F32)
    c1 = jnp.sum(oh1, axis=1, keepdims=True)
    c2 = jnp.sum(oh2, axis=1, keepdims=True)
    padded = jnp.floor((c1 + c2 + (MOE_TILE - 1)) / MOE_TILE) * MOE_TILE
    erow = lax.broadcasted_iota(jnp.int32, (ne, 1), 0)
    start = jnp.zeros((ne, 1), F32)
    for e in range(ne - 1):
        start = start + jnp.where(erow > e, padded[e:e + 1, :], 0.0)
    end = start + padded
    tri = (lax.broadcasted_iota(jnp.int32, (chunk, chunk), 0)
           < lax.broadcasted_iota(jnp.int32, (chunk, chunk), 1)).astype(BF16)
    carry = jnp.concatenate([jnp.zeros((ne, 1), F32), c1], axis=0)
    base1 = start
    for ci in range(m // chunk):
        sl = slice(ci * chunk, (ci + 1) * chunk)
        oh = jnp.concatenate([oh1[:, sl], oh2[:, sl]], axis=0)
        pre = jnp.dot(oh.astype(BF16), tri, preferred_element_type=F32) + carry
        carry = carry + jnp.sum(oh, axis=1, keepdims=True)
        s1 = jnp.sum(oh[:ne] * (pre[:ne] + base1), axis=0, keepdims=True)
        s2 = jnp.sum(oh[ne:] * (pre[ne:] + base1), axis=0, keepdims=True)
        slot1_ref[:, sl] = s1.astype(jnp.int32)
        slot2_ref[:, sl] = s2.astype(jnp.int32)
    tile0 = (lax.broadcasted_iota(jnp.int32, (1, V7X_LANES), 1) * MOE_TILE).astype(F32)
    te = jnp.sum((end <= tile0).astype(jnp.int32), axis=0, keepdims=True)
    nonempty = padded > 0.0
    last = jnp.max(jnp.where(nonempty, erow, 0), axis=0, keepdims=True)
    te = jnp.minimum(te, last)
    te_ref[...] = te
    tv_ref[...] = (tile0 < end[ne - 1:ne, :]).astype(jnp.int32)
    nx_ref[...] = jnp.min(jnp.where(nonempty & (erow > te), erow, ne), axis=0, keepdims=True)


def _route(logits_t, router_b, n_tiles):
    ne, m = logits_t.shape
    assert n_tiles <= V7X_LANES
    vec = lambda dt: jax.ShapeDtypeStruct((1, m), dt)
    lane = jax.ShapeDtypeStruct((1, V7X_LANES), jnp.int32)
    return pl.pallas_call(
        functools.partial(_route_kernel, chunk=_pick_tile(m, (512, 256, 128))),
        out_shape=[vec(jnp.int32), vec(jnp.int32), vec(F32), vec(F32), lane, lane, lane],
        compiler_params=_params(None),
    )(logits_t, router_b.reshape(ne, 1))


def _row_copy(src_hbm, dst_vmem, sem, src_row, dst_row):
    return pltpu.make_async_copy(src_hbm.at[pl.ds(src_row, 1)], dst_vmem.at[pl.ds(dst_row, 1)], sem)


def _norm_scatter_kernel(s1_ref, s2_ref, h_ref, g_ref, xs_in_hbm, xs_hbm, pk_ref, sem, *, n_steps):
    del xs_in_hbm
    t = pl.program_id(0) * pl.num_programs(1) + pl.program_id(1)
    cur = t % 2
    tm = h_ref.shape[0]
    x = h_ref[...]
    ms = jnp.mean(x * x, axis=-1, keepdims=True)
    pk_ref[cur] = x * lax.rsqrt(ms + EPS) * g_ref[...]

    def row_out(buf, r, slot):
        return pltpu.make_async_copy(pk_ref.at[buf, pl.ds(r, 1)], xs_hbm.at[pl.ds(slot, 1)], sem.at[buf])

    def start(r4, carry):
        for k in range(ROW_DMA_UNROLL):
            r = r4 * ROW_DMA_UNROLL + k
            row_out(cur, r, s1_ref[0, 0, r]).start()
            row_out(cur, r, s2_ref[0, 0, r]).start()
        return carry

    lax.fori_loop(0, tm // ROW_DMA_UNROLL, start, 0)

    def wait_all(buf):
        def wait(r, carry):
            row_out(buf, r, 0).wait()
            row_out(buf, r, 0).wait()
            return carry

        lax.fori_loop(0, tm, wait, 0)

    @pl.when(t > 0)
    def _():
        wait_all(1 - cur)

    @pl.when(t == n_steps - 1)
    def _():
        wait_all(cur)


def _norm_scatter(h3, g, slot1, slot2, n_real, n_slots, tm=256):
    b, _, d = h3.shape
    nt = n_real // tm
    idx = lambda a: a.reshape(b * nt, 1, tm)
    smem_blk = pl.BlockSpec((1, 1, tm), lambda bi, i: (bi * nt + i, 0, 0), memory_space=pltpu.SMEM)
    return pl.pallas_call(
        functools.partial(_norm_scatter_kernel, n_steps=b * nt),
        grid=(b, nt),
        in_specs=[smem_blk, smem_blk,
                  pl.BlockSpec((None, tm, d), lambda bi, i: (bi, i, 0)),
                  pl.BlockSpec((1, d), lambda bi, i: (0, 0)),
                  pl.BlockSpec(memory_space=pl.ANY)],
        out_specs=pl.BlockSpec(memory_space=pl.ANY),
        out_shape=jax.ShapeDtypeStruct((n_slots, d), F32),
        scratch_shapes=[pltpu.VMEM((2, tm, d), F32), pltpu.SemaphoreType.DMA((2,))],
        input_output_aliases={4: 0},
        compiler_params=_params(("arbitrary", "arbitrary")),
    )(idx(slot1), idx(slot2), h3, g.reshape(1, d), jnp.zeros((n_slots, d), F32))


def _combine_kernel(s1_ref, s2_ref, n1_ref, n2_ref, h_ref, g1_ref, g2_ref, fg_ref, y_hbm, o_ref,
                    a_ref, b_ref, sem, *, n_steps):
    t = pl.program_id(0) * pl.num_programs(1) + pl.program_id(1)
    cur = t % 2
    n = a_ref.shape[1]

    def gather(buf, i1_ref, i2_ref):
        def start(r4, carry):
            for k in range(ROW_DMA_UNROLL):
                r = r4 * ROW_DMA_UNROLL + k
                _row_copy(y_hbm, a_ref.at[buf], sem.at[buf], i1_ref[0, 0, r], r).start()
                _row_copy(y_hbm, b_ref.at[buf], sem.at[buf], i2_ref[0, 0, r], r).start()
            return carry

        lax.fori_loop(0, n // ROW_DMA_UNROLL, start, 0)

    @pl.when(t == 0)
    def _():
        gather(0, s1_ref, s2_ref)

    @pl.when(t + 1 < n_steps)
    def _():
        gather(1 - cur, n1_ref, n2_ref)

    def wait(r, carry):
        _row_copy(y_hbm, a_ref.at[cur], sem.at[cur], 0, r).wait()
        _row_copy(y_hbm, b_ref.at[cur], sem.at[cur], 0, r).wait()
        return carry

    lax.fori_loop(0, n, wait, 0)
    h = h_ref[...] + g1_ref[...] * a_ref[cur] + g2_ref[...] * b_ref[cur]
    ms = jnp.mean(h * h, axis=-1, keepdims=True)
    o_ref[...] = h * lax.rsqrt(ms + EPS) * fg_ref[...]


def _moe_combine_final(h3, y, slot1, slot2, g1, g2, final_g, n_real, tm=256):
    b, _, d = h3.shape
    nt = n_real // tm
    idx = lambda a: a.reshape(b * nt, 1, tm)
    gate = lambda a: a.reshape(b, n_real, 1)
    smem_blk = pl.BlockSpec((1, 1, tm), lambda bi, i: (bi * nt + i, 0, 0), memory_space=pltpu.SMEM)
    smem_next = pl.BlockSpec((1, 1, tm), lambda bi, i: (jnp.minimum(bi * nt + i + 1, b * nt - 1), 0, 0),
                             memory_space=pltpu.SMEM)
    return pl.pallas_call(
        functools.partial(_combine_kernel, n_steps=b * nt),
        grid=(b, nt),
        in_specs=[smem_blk, smem_blk, smem_next, smem_next,
                  pl.BlockSpec((None, tm, d), lambda bi, i: (bi, i, 0)),
                  pl.BlockSpec((None, tm, 1), lambda bi, i: (bi, i, 0)),
                  pl.BlockSpec((None, tm, 1), lambda bi, i: (bi, i, 0)),
                  pl.BlockSpec((1, d), lambda bi, i: (0, 0)),
                  pl.BlockSpec(memory_space=pl.ANY)],
        out_specs=pl.BlockSpec((None, tm, d), lambda bi, i: (bi, i, 0)),
        out_shape=jax.ShapeDtypeStruct((b, n_real, d), F32),
        scratch_shapes=[pltpu.VMEM((2, tm, d), F32), pltpu.VMEM((2, tm, d), F32),
                        pltpu.SemaphoreType.DMA((2,))],
        compiler_params=_params(("arbitrary", "arbitrary")),
    )(idx(slot1), idx(slot2), idx(slot1), idx(slot2), h3, gate(g1), gate(g2), final_g.reshape(1, d), y)


def _filter_features(l, lhp, n_meta, n_emb):
    j = np.arange(lhp)
    r = np.concatenate([2 * j, 2 * j + 1])
    lp = 2 * lhp
    pos = (r + n_meta) % l
    valid = r < l
    t = pos / (l - 1.0)
    bands = (n_emb - 1) // 2
    w = 2.0 * np.pi * pos / l
    f = np.linspace(1e-4, bands - 1, bands)
    z = np.zeros((lp, V7X_LANES), np.float64)
    z[:, 0] = t
    z[:, 1:1 + bands] = np.cos(f[None, :] * w[:, None])
    z[:, 1 + bands:1 + 2 * bands] = -np.sin(f[None, :] * w[:, None])
    z *= valid[:, None]
    z[:, V7X_LANES - 1] = valid
    return jnp.asarray(z, F32)


def kernel(x, meta_tokens, norm_mix_g, w_in, hy_conv_w, hy_conv_b, hy_f_w1, hy_f_b1, hy_f_w2, hy_f_b2, hy_f_w3, hy_f_b3, hy_f_wo, hy_f_freq, hy_decay, hy_skip, cv_dw_w, cv_dw_b, cv_ln_g, cv_ln_b, attn_sink, group_norm_g, w_out, norm_ffn_g, ffn_w1, ffn_w3, ffn_w2, router_w, router_b, moe_w1, moe_w3, moe_w2, final_norm_g):
    b, n_real, d = x.shape
    n_meta = meta_tokens.shape[0]
    depth = w_in.shape[0]
    l = n_real + n_meta
    m = b * l
    lp = _round_up(l, 3 * V7X_LANES)
    w_hy = hy_skip.shape[1]
    w_fn = w_hy
    w_cv = cv_dw_b.shape[1]
    kv_w = N_KV_HEADS * HEAD_DIM
    p_in = w_in.shape[2]
    w_at = p_in - 3 * w_hy - w_fn - 2 * w_cv - 2 * kv_w
    n_q_heads = w_at // HEAD_DIM
    d_ff = ffn_w1.shape[2]
    assert depth == 2 and router_w.shape[2] == N_EXPERTS
    assert n_meta == V7X_BF16_ROWS and n_real % 256 == 0
    off_fn = 3 * w_hy
    off_cv = off_fn + w_fn
    off_q = off_cv + 2 * w_cv
    off_k = off_q + w_at
    off_v = off_k + kv_w

    lh = l // 2
    lhp = _round_up(lh + 1, V7X_LANES)
    j = np.arange(lhp)
    freq = np.arange(lhp)
    hy_unit, hy_period = np.pi / l, 2 * l
    tabs_f, tabs_t = [], []
    for par in (0, 1):
        col_pos = (2 * j + par + n_meta) % l
        tabs_f.append(_dft_tables(col_pos, j < lh, hy_unit, hy_period, 0, 1, lhp // 16 - 1,
                                  np.arange(lhp - 16, lhp), lhp))
        tabs_t.append(_dft_tables(freq, freq <= lh, hy_unit, hy_period, n_meta + par, 2, n_real // 32,
                                  np.arange(par, n_meta, 2), lhp))
    r = np.arange(lp)
    cos_l, sin_l = _dft_tables((r + n_meta) % l, r < l, 2 * np.pi / l, l, n_meta, 1, n_real // 16,
                               np.arange(n_meta), lp)
    edge = np.where(freq == 0, 1.0, 2.0) / (2.0 * l)
    w1 = jnp.asarray(np.where(freq <= lh, edge, 0.0).reshape(lhp, 1), F32)
    w2 = jnp.asarray(np.where(freq < lh, edge, 0.0).reshape(lhp, 1), F32)
    zc = _filter_features(l, lhp, n_meta, hy_f_w1.shape[1])

    h = jnp.concatenate([x, jnp.broadcast_to(meta_tokens[None].astype(x.dtype), (b, n_meta, d))], axis=1)
    h = h.reshape(m, d)
    out = None
    for layer in range(depth):
        xn = _rmsnorm(h, norm_mix_g[layer], BF16)
        proj = _matmul(xn, [w_in], layer, tk=d, k_blk=0, n_out=p_in, tn=512, tm=1040, out_dtype=F32)
        proj3 = proj.reshape(b, l, p_in)

        hs, hd = _hyena_filter_sums(zc, hy_f_w1[layer], hy_f_b1[layer], hy_f_w2[layer], hy_f_b2[layer],
                                    hy_f_w3[layer], hy_f_b3[layer], hy_f_wo[layer], hy_f_freq[layer],
                                    hy_decay[layer], w_hy)
        kspec = _hyena_filter_spectrum(tabs_f[0], tabs_f[1], hs, hd, w1, w2)
        x0c, vxe, vxo = _hyena_prep(proj3, hy_conv_w[layer], hy_conv_b[layer], w_hy, n_real, n_meta, lhp)
        spec = _hyena_fwd(tabs_f[0], tabs_f[1], vxe, vxo, kspec)
        y_a = _hyena_inv(tabs_t[0], tabs_t[1], spec, vxe, vxo, x0c, hy_skip[layer])

        fa, fb = _fnet_chan(proj3, off_fn // w_fn, w_fn, lp)
        y_b = _fnet_time(cos_l, sin_l, fa, fb, l)

        y_c = _conformer_conv(proj3, off_cv // 128, cv_dw_w[layer], cv_dw_b[layer], w_cv, n_real, n_meta)
        y_d = _windowed_attention(proj3, attn_sink[layer], off_q, off_k, off_v, n_q_heads, n_real, n_meta)

        ycat = _group_norms(y_a.reshape(m, w_hy), y_b.reshape(m, w_fn), y_c.reshape(m, w_cv),
                            y_d.reshape(m, w_at), group_norm_g[layer], cv_ln_g[layer], cv_ln_b[layer])
        h = _matmul(ycat, [w_out], layer, tk=ycat.shape[1], k_blk=0, n_out=d, tn=512, tm=1040,
                    out_dtype=F32, residual=h)

        if layer % 2 == 0:
            i = layer // 2
            xn = _rmsnorm(h, norm_ffn_g[layer], BF16)
            tn_ff = _pick_tile(d_ff, (512, 256, 128))
            act = _matmul(xn, [ffn_w1, ffn_w3], i, tk=d, k_blk=0, n_out=d_ff, tn=tn_ff, tm=1040,
                          out_dtype=BF16, swiglu=True)
            n_k = 2 if d_ff % (2 * V7X_LANES) == 0 and d_ff > 4096 else 1
            for kb in range(n_k):
                h = _matmul(act, [ffn_w2], i, tk=d_ff // n_k, k_blk=kb, n_out=d, tn=512, tm=520,
                            out_dtype=F32, residual=h)
        else:
            i = layer // 2
            h3 = h.reshape(b, l, d)
            mr = b * n_real
            logits_t = _norm_router(h3, norm_ffn_g[layer], router_w[i], n_real)
            n_slots = _round_up(2 * mr + N_EXPERTS * (MOE_TILE - 1), MOE_TILE)
            n_tiles = n_slots // MOE_TILE
            slot1, slot2, g1, g2, te, tv, nx = _route(logits_t, router_b[i], n_tiles)
            slot1, slot2 = slot1.reshape(mr), slot2.reshape(mr)
            te, tv, nx = (a.reshape(-1)[:n_tiles] for a in (te, tv, nx))
            xs = _norm_scatter(h3, norm_ffn_g[layer], slot1, slot2, n_real, n_slots)
            act = _grouped_matmul(xs, [moe_w1[i], moe_w3[i]], te, tv, nx, tn=512, out_dtype=BF16, swiglu=True)
            ys = _grouped_matmul(act, [moe_w2[i]], te, tv, nx, tn=1024, out_dtype=F32)
            out = _moe_combine_final(h3, ys, slot1, slot2, g1, g2, final_norm_g, n_real)
    return out
```

```python
import functools
import math

import numpy as np
import jax
import jax.numpy as jnp
from jax import lax
from jax.experimental import pallas as pl
from jax.experimental.pallas import tpu as pltpu

F32 = jnp.float32
BF16 = jnp.bfloat16

V7X_LANES = 128
V7X_BF16_ROWS = 16
V7X_VMEM_LIMIT = 58 * 1024 * 1024

EPS = 1e-6
NEG = -1e30
HEAD_DIM = 128
N_KV_HEADS = 2
ATT_BLOCK = 128
FNET_GROUP_W = 256
N_EXPERTS = 8
MOE_TILE = 512
ROW_DMA_UNROLL = 4
MM_ROWS = 1376


def _round_up(x, m):
    return (x + m - 1) // m * m


def _pick_tile(n, candidates):
    for c in candidates:
        if n % c == 0:
            return c
    raise ValueError(f"no tile for {n} in {candidates}")


def _params(sem, vmem=None):
    if sem is None:
        return pltpu.CompilerParams(vmem_limit_bytes=vmem or V7X_VMEM_LIMIT)
    return pltpu.CompilerParams(dimension_semantics=sem, vmem_limit_bytes=vmem or V7X_VMEM_LIMIT)


def _rmsnorm_kernel(x_ref, g_ref, o_ref):
    x = x_ref[...]
    ms = jnp.mean(x * x, axis=-1, keepdims=True)
    o_ref[...] = (x * lax.rsqrt(ms + EPS) * g_ref[...]).astype(o_ref.dtype)


def _rmsnorm(x2d, g, out_dtype, tm=256):
    m, d = x2d.shape
    return pl.pallas_call(
        _rmsnorm_kernel,
        grid=(pl.cdiv(m, tm),),
        in_specs=[pl.BlockSpec((tm, d), lambda i: (i, 0)), pl.BlockSpec((1, d), lambda i: (0, 0))],
        out_specs=pl.BlockSpec((tm, d), lambda i: (i, 0)),
        out_shape=jax.ShapeDtypeStruct((m, d), out_dtype),
        compiler_params=_params(("arbitrary",)),
    )(x2d, g.reshape(1, d))


def _cast_tile(w_ref, wb_ref):
    k = w_ref.shape[0]
    ck = _pick_tile(k, (256, 128, 64, 32, 16))

    def body(c, carry):
        r = pl.multiple_of(c * ck, ck)
        wb_ref[pl.ds(r, ck), :] = w_ref[pl.ds(r, ck), :].astype(BF16)
        return carry

    lax.fori_loop(0, k // ck, body, 0)


def _mm_kernel(x_ref, *refs, n_w, swiglu, has_res, layer, k_off, tn):
    w_hbm = refs[:n_w]
    res_ref = refs[n_w] if has_res else None
    o_ref, stage_ref, wb_ref, sem = refs[n_w + int(has_res):n_w + int(has_res) + 4]
    j = pl.program_id(0)
    i = pl.program_id(1)
    tk = stage_ref.shape[1]

    def w_copy(idx, col_tile):
        col = pl.multiple_of(col_tile * tn, tn)
        return pltpu.make_async_copy(w_hbm[idx].at[layer, pl.ds(k_off, tk), pl.ds(col, tn)],
                                     stage_ref.at[idx], sem.at[idx])

    @pl.when((j == 0) & (i == 0))
    def _():
        for idx in range(n_w):
            w_copy(idx, 0).start()

    @pl.when(i == 0)
    def _():
        for idx in range(n_w):
            w_copy(idx, j).wait()
            _cast_tile(stage_ref.at[idx], wb_ref.at[idx])

        @pl.when(j + 1 < pl.num_programs(0))
        def _():
            for idx in range(n_w):
                w_copy(idx, j + 1).start()

    x = x_ref[...]
    acc = jnp.dot(x, wb_ref[0], preferred_element_type=F32)
    if swiglu:
        acc = acc * jax.nn.sigmoid(acc) * jnp.dot(x, wb_ref[1], preferred_element_type=F32)
    if has_res:
        acc = acc + res_ref[...]
    o_ref[...] = acc.astype(o_ref.dtype)


def _matmul(x, ws, layer, *, tk, k_blk, n_out, tn, tm, out_dtype, residual=None, swiglu=False):
    m = x.shape[0]
    n_w = len(ws)
    tm = min(tm, _round_up(m, V7X_BF16_ROWS))
    in_specs = [pl.BlockSpec((tm, tk), lambda j, i: (i, k_blk))]
    in_specs += [pl.BlockSpec(memory_space=pl.ANY) for _ in ws]
    args = [x, *ws]
    if residual is not None:
        in_specs.append(pl.BlockSpec((tm, tn), lambda j, i: (i, j)))
        args.append(residual)
    kern = functools.partial(_mm_kernel, n_w=n_w, swiglu=swiglu, has_res=residual is not None,
                             layer=layer, k_off=k_blk * tk, tn=tn)
    return pl.pallas_call(
        kern,
        grid=(n_out // tn, pl.cdiv(m, tm)),
        in_specs=in_specs,
        out_specs=pl.BlockSpec((tm, tn), lambda j, i: (i, j)),
        out_shape=jax.ShapeDtypeStruct((m, n_out), out_dtype),
        scratch_shapes=[pltpu.VMEM((n_w, tk, tn), F32), pltpu.VMEM((n_w, tk, tn), BF16),
                        pltpu.SemaphoreType.DMA((n_w,))],
        compiler_params=_params(("arbitrary", "arbitrary")),
    )(*args)


def _gmm_kernel(te_ref, tv_ref, nx_ref, x_ref, *refs, n_w, swiglu, tn):
    w_hbm = refs[:n_w]
    o_ref, stage_ref, wb_ref, sem = refs[n_w:n_w + 4]
    j = pl.program_id(0)
    t = pl.program_id(1)
    n_j = pl.num_programs(0)
    e = te_ref[t]

    def w_copy(i, expert, col_tile):
        col = pl.multiple_of(col_tile * tn, tn)
        return pltpu.make_async_copy(w_hbm[i].at[expert, :, pl.ds(col, tn)], stage_ref.at[i], sem.at[i])

    @pl.when((j == 0) & (t == 0))
    def _():
        for i in range(n_w):
            w_copy(i, e, 0).start()

    @pl.when((t == 0) | (e != te_ref[jnp.maximum(t - 1, 0)]))
    def _():
        for i in range(n_w):
            w_copy(i, e, j).wait()
            _cast_tile(stage_ref.at[i], wb_ref.at[i])
        e_next = nx_ref[t]

        @pl.when(e_next < N_EXPERTS)
        def _():
            for i in range(n_w):
                w_copy(i, e_next, j).start()

        @pl.when((e_next >= N_EXPERTS) & (j + 1 < n_j))
        def _():
            for i in range(n_w):
                w_copy(i, te_ref[0], j + 1).start()

    @pl.when(tv_ref[t] != 0)
    def _():
        x = x_ref[...].astype(BF16)
        acc = jnp.dot(x, wb_ref[0], preferred_element_type=F32)
        if swiglu:
            acc = acc * jax.nn.sigmoid(acc) * jnp.dot(x, wb_ref[1], preferred_element_type=F32)
        o_ref[...] = acc.astype(o_ref.dtype)

    @pl.when(tv_ref[t] == 0)
    def _():
        o_ref[...] = jnp.zeros_like(o_ref)


def _grouped_matmul(x, ws, tile_expert, tile_valid, next_expert, *, tn, out_dtype, swiglu=False):
    s, k = x.shape
    n = ws[0].shape[2]
    n_w = len(ws)
    grid_spec = pltpu.PrefetchScalarGridSpec(
        num_scalar_prefetch=3,
        grid=(n // tn, s // MOE_TILE),
        in_specs=[pl.BlockSpec((MOE_TILE, k), lambda j, t, te, tv, nx: (t, 0))]
        + [pl.BlockSpec(memory_space=pl.ANY) for _ in ws],
        out_specs=pl.BlockSpec((MOE_TILE, tn), lambda j, t, te, tv, nx: (t, j)),
        scratch_shapes=[pltpu.VMEM((n_w, k, tn), F32), pltpu.VMEM((n_w, k, tn), BF16),
                        pltpu.SemaphoreType.DMA((n_w,))],
    )
    return pl.pallas_call(
        functools.partial(_gmm_kernel, n_w=n_w, swiglu=swiglu, tn=tn),
        grid_spec=grid_spec,
        out_shape=jax.ShapeDtypeStruct((s, n), out_dtype),
        compiler_params=_params(("arbitrary", "arbitrary")),
    )(tile_expert, tile_valid, next_expert, x, *ws)


def _dft_table_kernel(ac_ref, as_ref, bc_ref, bs_ref, cc_ref, cs_ref, tc_ref, ts_ref, c_ref, s_ref, *,
                      n_blocks, n_rows_out):
    ac = ac_ref[...]
    asn = as_ref[...]

    def body(q, carry):
        v = q % 16
        u = q // 16
        bc, bs = bc_ref[pl.ds(v, 1), :], bs_ref[pl.ds(v, 1), :]
        cc, cs = cc_ref[pl.ds(u, 1), :], cs_ref[pl.ds(u, 1), :]
        dc = bc * cc - bs * cs
        dsn = bs * cc + bc * cs
        row = pl.multiple_of(q * 16, 16)
        c_ref[pl.ds(row, 16), :] = (ac * dc - asn * dsn).astype(BF16)
        s_ref[pl.ds(row, 16), :] = (asn * dc + ac * dsn).astype(BF16)
        return carry

    lax.fori_loop(0, n_blocks, body, 0)
    c_ref[pl.ds(16 * n_blocks, 16), :] = tc_ref[...].astype(BF16)
    s_ref[pl.ds(16 * n_blocks, 16), :] = ts_ref[...].astype(BF16)
    pad = n_rows_out - 16 * (n_blocks + 1)
    if pad:
        c_ref[pl.ds(16 * (n_blocks + 1), pad), :] = jnp.zeros((pad, c_ref.shape[1]), BF16)
        s_ref[pl.ds(16 * (n_blocks + 1), pad), :] = jnp.zeros((pad, s_ref.shape[1]), BF16)


def _dft_tables(col_pos, col_valid, unit, period, row0, row_step, n_blocks, tail_pos, n_rows_out):
    ncol = col_pos.shape[0]
    n_u = -(-n_blocks // 16)
    p = col_pos.astype(np.int64)[None, :]
    valid = col_valid[None, :].astype(np.float64)
    assert len(tail_pos) <= 16 and n_rows_out >= 16 * (n_blocks + 1)

    def factor(rowpos, mask):
        ang = unit * ((np.asarray(rowpos, dtype=np.int64)[:, None] * p) % period)
        return (jnp.asarray((np.cos(ang) * mask).astype(np.float32)),
                jnp.asarray((np.sin(ang) * mask).astype(np.float32)))

    a_c, a_s = factor(row0 + row_step * np.arange(16), valid)
    b_c, b_s = factor(row_step * 16 * np.arange(16), 1.0)
    c_c, c_s = factor(row_step * 256 * np.arange(n_u), 1.0)
    tail = np.zeros(16, np.int64)
    tail[:len(tail_pos)] = tail_pos
    t_c, t_s = factor(tail, valid * (np.arange(16) < len(tail_pos))[:, None])
    pc = _pick_tile(ncol, (2176, 1408, 384, 256, 128))
    kern = functools.partial(_dft_table_kernel, n_blocks=n_blocks, n_rows_out=n_rows_out)
    col = lambda rows: pl.BlockSpec((rows, pc), lambda j: (0, j))
    return pl.pallas_call(
        kern,
        grid=(ncol // pc,),
        in_specs=[col(16), col(16), col(16), col(16), col(n_u), col(n_u), col(16), col(16)],
        out_specs=[col(n_rows_out), col(n_rows_out)],
        out_shape=[jax.ShapeDtypeStruct((n_rows_out, ncol), BF16)] * 2,
        compiler_params=_params(("arbitrary",)),
    )(a_c, a_s, b_c, b_s, c_c, c_s, t_c, t_s)


def _filter_kernel(z_ref, w1_ref, b1_ref, w2_ref, b2_ref, w3_ref, b3_ref, wo_ref, fr_ref, dec_ref,
                   hs_ref, hd_ref, *, width):
    hp = lax.Precision.HIGHEST
    z = z_ref[...]
    fr = fr_ref[...]
    hid = jnp.sin(fr * (jnp.dot(z, w1_ref[...], precision=hp, preferred_element_type=F32) + b1_ref[...]))
    hid = jnp.sin(fr * (jnp.dot(hid, w2_ref[...], precision=hp, preferred_element_type=F32) + b2_ref[...]))
    hid = jnp.sin(fr * (jnp.dot(hid, w3_ref[...], precision=hp, preferred_element_type=F32) + b3_ref[...]))
    h = jnp.dot(hid, wo_ref[...], precision=hp, preferred_element_type=F32)
    t = z[:, 0:1]
    valid = z[:, V7X_LANES - 1:V7X_LANES]
    h = h * jnp.exp(-t * dec_ref[...]) * valid
    hf = h[:, :width]
    hb = h[:, width:]
    hs_ref[...] = (hf + hb).astype(BF16)
    hd_ref[...] = (hf - hb).astype(BF16)


def _hyena_filter_sums(zc, w1, b1, w2, b2, w3, b3, wo, freq, decay, width):
    lp = zc.shape[0]
    hid = w2.shape[0]
    tr = _pick_tile(lp, (528, 384, 256, 128))
    w1p = jnp.zeros((V7X_LANES, hid), F32).at[:w1.shape[0]].set(w1)
    full = lambda shape: pl.BlockSpec(shape, lambda i: (0, 0))
    return pl.pallas_call(
        functools.partial(_filter_kernel, width=width),
        grid=(lp // tr,),
        in_specs=[pl.BlockSpec((tr, V7X_LANES), lambda i: (i, 0)),
                  full((V7X_LANES, hid)), full((1, hid)), full((hid, hid)), full((1, hid)),
                  full((hid, hid)), full((1, hid)), full((hid, 2 * width)), full((1, hid)),
                  full((1, 2 * width))],
        out_specs=[pl.BlockSpec((tr, width), lambda i: (i, 0))] * 2,
        out_shape=[jax.ShapeDtypeStruct((lp, width), BF16)] * 2,
        compiler_params=_params(("arbitrary",)),
    )(zc, w1p, b1.reshape(1, hid), w2, b2.reshape(1, hid), w3, b3.reshape(1, hid), wo,
      freq.reshape(1, hid), decay.reshape(1, 2 * width))


def _kf_kernel(ce_ref, se_ref, co_ref, so_ref, hse_ref, hso_ref, hde_ref, hdo_ref, w1_ref, w2_ref,
               k1re_ref, k1im_ref, k2re_ref, k2im_ref):
    ka = jnp.dot(ce_ref[...], hse_ref[...], preferred_element_type=F32)
    kb = jnp.dot(co_ref[...], hso_ref[...], preferred_element_type=F32)
    sa = jnp.dot(se_ref[...], hde_ref[...], preferred_element_type=F32)
    sb = jnp.dot(so_ref[...], hdo_ref[...], preferred_element_type=F32)
    w1 = w1_ref[...]
    w2 = w2_ref[...]
    k1re_ref[...] = w1 * (ka + kb)
    k1im_ref[...] = -w1 * (sa + sb)
    k2re_ref[...] = w2 * (ka - kb)
    k2im_ref[...] = w2 * (sa - sb)


def _half_tile(n):
    return _pick_tile(n, (544, 384, 256, 128))


def _hyena_filter_spectrum(tabs_e, tabs_o, hs, hd, w1, w2):
    (ce, se), (co, so) = tabs_e, tabs_o
    f2p, lhp = ce.shape
    width = hs.shape[1]
    tf = _half_tile(f2p)
    cb = 512
    tab = pl.BlockSpec((tf, lhp), lambda c, f: (f, 0))
    half = lambda which: pl.BlockSpec((lhp, cb), lambda c, f: (which, c))
    wcol = pl.BlockSpec((tf, 1), lambda c, f: (f, 0))
    return pl.pallas_call(
        _kf_kernel,
        grid=(width // cb, f2p // tf),
        in_specs=[tab, tab, tab, tab, half(0), half(1), half(0), half(1), wcol, wcol],
        out_specs=[pl.BlockSpec((tf, cb), lambda c, f: (f, c))] * 4,
        out_shape=[jax.ShapeDtypeStruct((f2p, width), F32)] * 4,
        compiler_params=_params(("arbitrary", "arbitrary")),
    )(ce, se, co, so, hs, hs, hd, hd, w1, w2)


def _hyena_prep_kernel(x0_ref, x1_ref, v_ref, w0_ref, w1_ref, wv_ref, b0_ref, b1_ref, bv_ref,
                       x0c_ref, vxe_ref, vxo_ref, pad_ref, vx_ref, *, n_real, n_meta):
    cb = pad_ref.shape[1]
    l = n_real + n_meta
    pad_ref[pl.ds(0, 8), :] = jnp.zeros((8, cb), F32)
    pad_ref[pl.ds(8 + l, 8), :] = jnp.zeros((8, cb), F32)

    def conv(u_ref, w_ref, b_ref):
        pad_ref[pl.ds(8, n_meta), :] = u_ref[pl.ds(n_real, n_meta), :].astype(F32)
        pad_ref[pl.ds(8 + n_meta, n_real), :] = u_ref[pl.ds(0, n_real), :].astype(F32)
        w = w_ref[...]
        b = b_ref[...]

        def at(start, n):
            return (w[0:1] * pad_ref[pl.ds(start - 1, n), :] + w[1:2] * pad_ref[pl.ds(start, n), :]
                    + w[2:3] * pad_ref[pl.ds(start + 1, n), :] + b)

        return at(8 + n_meta, n_real), at(8, n_meta)

    x0_real, x0_meta = conv(x0_ref, w0_ref, b0_ref)
    x0c_ref[pl.ds(0, n_real), :] = x0_real
    x0c_ref[pl.ds(n_real, n_meta), :] = x0_meta
    x1_real, x1_meta = conv(x1_ref, w1_ref, b1_ref)
    v_real, v_meta = conv(v_ref, wv_ref, bv_ref)
    vx_ref[pl.ds(0, n_real), :] = v_real * x1_real
    vx_ref[pl.ds(n_real, n_meta), :] = v_meta * x1_meta
    lh = l // 2
    lhp = vxe_ref.shape[0]
    for par, o_ref in ((0, vxe_ref), (1, vxo_ref)):
        o_ref[pl.ds(0, lh), :] = vx_ref[pl.ds(par, lh, stride=2), :].astype(BF16)
        o_ref[pl.ds(lh, lhp - lh), :] = jnp.zeros((lhp - lh, cb), BF16)


def _hyena_prep(proj3, conv_w, conv_b, width, n_real, n_meta, lhp):
    b, l, _ = proj3.shape
    cb = 128
    nc = width // cb
    col = lambda off: pl.BlockSpec((None, l, cb), lambda bi, c: (bi, 0, off + c))
    wcol = lambda off: pl.BlockSpec((3, cb), lambda bi, c: (0, off + c))
    bcol = lambda off: pl.BlockSpec((1, cb), lambda bi, c: (0, off + c))
    cbias = conv_b.reshape(1, 3 * width)
    half = pl.BlockSpec((None, lhp, cb), lambda bi, c: (bi, 0, c))
    return pl.pallas_call(
        functools.partial(_hyena_prep_kernel, n_real=n_real, n_meta=n_meta),
        grid=(b, nc),
        in_specs=[col(0), col(nc), col(2 * nc), wcol(0), wcol(nc), wcol(2 * nc),
                  bcol(0), bcol(nc), bcol(2 * nc)],
        out_specs=[pl.BlockSpec((None, l, cb), lambda bi, c: (bi, 0, c)), half, half],
        out_shape=[jax.ShapeDtypeStruct((b, l, width), F32)]
        + [jax.ShapeDtypeStruct((b, lhp, width), BF16)] * 2,
        scratch_shapes=[pltpu.VMEM((l + 16, cb), F32), pltpu.VMEM((l, cb), F32)],
        compiler_params=_params(("arbitrary", "arbitrary")),
    )(proj3, proj3, proj3, conv_w, conv_w, conv_w, cbias, cbias, cbias)


def _hyena_fwd_kernel(ce_ref, se_ref, co_ref, so_ref, vxe_ref, vxo_ref, k1re_ref, k1im_ref, k2re_ref, k2im_ref,
                      ere_ref, eim_ref, ore_ref, oim_ref):
    vxe = vxe_ref[...]
    vxo = vxo_ref[...]
    a_re = jnp.dot(ce_ref[...], vxe, preferred_element_type=F32)
    b_re = jnp.dot(co_ref[...], vxo, preferred_element_type=F32)
    a_im = jnp.dot(se_ref[...], vxe, preferred_element_type=F32)
    b_im = jnp.dot(so_ref[...], vxo, preferred_element_type=F32)
    u1re, u1im = a_re + b_re, a_im + b_im
    u2re, u2im = a_re - b_re, b_im - a_im
    k1re, k1im, k2re, k2im = k1re_ref[...], k1im_ref[...], k2re_ref[...], k2im_ref[...]
    p_re = u1re * k1re + u1im * k1im
    p_im = u1re * k1im - u1im * k1re
    q_re = u2re * k2re + u2im * k2im
    q_im = u2re * k2im - u2im * k2re
    ere_ref[...] = (p_re + q_re).astype(BF16)
    eim_ref[...] = (p_im - q_im).astype(BF16)
    ore_ref[...] = (p_re - q_re).astype(BF16)
    oim_ref[...] = (p_im + q_im).astype(BF16)


def _hyena_fwd(tabs_e, tabs_o, vxe, vxo, kspec):
    (ce, se), (co, so) = tabs_e, tabs_o
    f2p, lhp = ce.shape
    b, _, width = vxe.shape
    tf = _half_tile(f2p)
    cb = 512
    tab = pl.BlockSpec((tf, lhp), lambda c, bi, f: (f, 0))
    vx_blk = pl.BlockSpec((None, lhp, cb), lambda c, bi, f: (bi, 0, c))
    k_blk = pl.BlockSpec((tf, cb), lambda c, bi, f: (f, c))
    return pl.pallas_call(
        _hyena_fwd_kernel,
        grid=(width // cb, b, f2p // tf),
        in_specs=[tab, tab, tab, tab, vx_blk, vx_blk, k_blk, k_blk, k_blk, k_blk],
        out_specs=[pl.BlockSpec((None, tf, cb), lambda c, bi, f: (bi, f, c))] * 4,
        out_shape=[jax.ShapeDtypeStruct((b, f2p, width), BF16)] * 4,
        compiler_params=_params(("arbitrary", "arbitrary", "arbitrary")),
    )(ce, se, co, so, vxe, vxo, *kspec)


def _hyena_inv_kernel(cet_ref, set_ref, cot_ref, sot_ref, ere_ref, eim_ref, ore_ref, oim_ref,
                      vxe_ref, vxo_ref, x0c_ref, skip_ref, o_ref, mix_ref):
    tr = vxe_ref.shape[0]
    skip = skip_ref[...]
    ye = (jnp.dot(cet_ref[...], ere_ref[...], preferred_element_type=F32)
          - jnp.dot(set_ref[...], eim_ref[...], preferred_element_type=F32))
    yo = (jnp.dot(cot_ref[...], ore_ref[...], preferred_element_type=F32)
          - jnp.dot(sot_ref[...], oim_ref[...], preferred_element_type=F32))
    te = ye + skip * vxe_ref[...].astype(F32)
    to = yo + skip * vxo_ref[...].astype(F32)
    for c in range(o_ref.shape[1] // V7X_LANES):
        lanes = slice(c * V7X_LANES, (c + 1) * V7X_LANES)
        mix_ref[c, pl.ds(0, tr, stride=2), :] = te[:, lanes]
        mix_ref[c, pl.ds(1, tr, stride=2), :] = to[:, lanes]
        o_ref[:, lanes] = mix_ref[c] * x0c_ref[:, lanes]


def _hyena_inv(tabs_et, tabs_ot, spec, vxe, vxo, x0c, skip):
    (cet, set_), (cot, sot) = tabs_et, tabs_ot
    lhp, f2p = cet.shape
    b, l, width = x0c.shape
    tr = _half_tile(lhp)
    cb = 512
    tab = pl.BlockSpec((tr, f2p), lambda c, bi, r: (r, 0))
    s_blk = pl.BlockSpec((None, f2p, cb), lambda c, bi, r: (bi, 0, c))
    vx_blk = pl.BlockSpec((None, tr, cb), lambda c, bi, r: (bi, r, c))
    row_blk = pl.BlockSpec((None, 2 * tr, cb), lambda c, bi, r: (bi, r, c))
    return pl.pallas_call(
        _hyena_inv_kernel,
        grid=(width // cb, b, lhp // tr),
        in_specs=[tab, tab, tab, tab, s_blk, s_blk, s_blk, s_blk, vx_blk, vx_blk, row_blk,
                  pl.BlockSpec((1, cb), lambda c, bi, r: (0, c))],
        out_specs=row_blk,
        out_shape=jax.ShapeDtypeStruct((b, l, width), F32),
        scratch_shapes=[pltpu.VMEM((cb // V7X_LANES, 2 * tr, V7X_LANES), F32)],
        compiler_params=_params(("arbitrary", "arbitrary", "arbitrary")),
    )(cet, set_, cot, sot, *spec, vxe, vxo, x0c, skip.reshape(1, width))


def _fnet_chan_kernel(u_ref, cc_ref, sc_ref, ae_ref, be_ref, ao_ref, bo_ref, *, l, tr):
    row = pl.program_id(1) * tr + lax.broadcasted_iota(jnp.int32, (tr, 1), 0)
    u = jnp.where(row < l, u_ref[...], 0.0).astype(BF16)
    cc = cc_ref[...].astype(BF16)
    sc = sc_ref[...].astype(BF16)
    pick = lax.broadcasted_iota(jnp.int32, (tr // 2, tr), 1) - 2 * lax.broadcasted_iota(jnp.int32, (tr // 2, tr), 0)
    gw = FNET_GROUP_W
    for par, a_ref, b_ref in ((0, ae_ref, be_ref), (1, ao_ref, bo_ref)):
        sel = jnp.where(pick == par, 1.0, 0.0).astype(BF16)
        up = jnp.dot(sel, u, preferred_element_type=F32).astype(BF16)
        for g in range(u.shape[1] // gw):
            ug = up[:, g * gw:(g + 1) * gw]
            a_ref[:, g * gw:(g + 1) * gw] = jnp.dot(ug, cc, preferred_element_type=F32).astype(BF16)
            b_ref[:, g * gw:(g + 1) * gw] = jnp.dot(ug, sc, preferred_element_type=F32).astype(BF16)


def _fnet_chan(proj3, col_blk, width, lhp):
    b, l, _ = proj3.shape
    tr = 256
    k = np.arange(FNET_GROUP_W)
    ang = 2.0 * np.pi * ((k[:, None] * k[None, :]) % FNET_GROUP_W) / FNET_GROUP_W
    cc = jnp.asarray(np.cos(ang), F32)
    sc = jnp.asarray(np.sin(ang), F32)
    return pl.pallas_call(
        functools.partial(_fnet_chan_kernel, l=l, tr=tr),
        grid=(b, 2 * lhp // tr),
        in_specs=[pl.BlockSpec((None, tr, width), lambda bi, r: (bi, r, col_blk)),
                  pl.BlockSpec((FNET_GROUP_W, FNET_GROUP_W), lambda bi, r: (0, 0)),
                  pl.BlockSpec((FNET_GROUP_W, FNET_GROUP_W), lambda bi, r: (0, 0))],
        out_specs=[pl.BlockSpec((None, tr // 2, width), lambda bi, r: (bi, r, 0))] * 4,
        out_shape=[jax.ShapeDtypeStruct((b, lhp, width), BF16)] * 4,
        compiler_params=_params(("arbitrary", "arbitrary")),
    )(proj3, cc, sc)


def _fnet_time_kernel(ce_ref, se_ref, co_ref, so_ref, ae_ref, be_ref, ao_ref, bo_ref, o_ref, *, scale):
    p = (jnp.dot(ce_ref[...], ae_ref[...], preferred_element_type=F32)
         - jnp.dot(se_ref[...], be_ref[...], preferred_element_type=F32))
    q = (jnp.dot(co_ref[...], ao_ref[...], preferred_element_type=F32)
         - jnp.dot(so_ref[...], bo_ref[...], preferred_element_type=F32))
    o_ref[0] = (p + q) * scale
    o_ref[1] = (p - q) * scale


def _fnet_time(tabs_e, tabs_o, parts, l):
    (ce, se), (co, so) = tabs_e, tabs_o
    lhp = ce.shape[0]
    b, _, width = parts[0].shape
    tr = _half_tile(lhp)
    cb = 512
    scale = 1.0 / math.sqrt(l * FNET_GROUP_W)
    tab = pl.BlockSpec((tr, lhp), lambda c, bi, r: (r, 0))
    blk = pl.BlockSpec((None, lhp, cb), lambda c, bi, r: (bi, 0, c))
    return pl.pallas_call(
        functools.partial(_fnet_time_kernel, scale=scale),
        grid=(width // cb, b, lhp // tr),
        in_specs=[tab, tab, tab, tab, blk, blk, blk, blk],
        out_specs=pl.BlockSpec((None, 2, tr, cb), lambda c, bi, r: (bi, 0, r, c)),
        out_shape=jax.ShapeDtypeStruct((b, 2, l // 2, width), F32),
        compiler_params=_params(("arbitrary", "arbitrary", "arbitrary")),
    )(ce, se, co, so, *parts)


def _conformer_kernel(a_ref, g_ref, w_ref, b_ref, o_ref, pad_ref, *, n_real, n_meta, taps, chunk):
    cb = pad_ref.shape[1]
    half = taps // 2
    lead = _round_up(half, 8)
    l = n_real + n_meta
    pad_ref[pl.ds(0, lead), :] = jnp.zeros((lead, cb), F32)
    pad_ref[pl.ds(lead + l, lead), :] = jnp.zeros((lead, cb), F32)
    def glu(start, n):
        return (a_ref[pl.ds(start, n), :].astype(F32)
                * jax.nn.sigmoid(g_ref[pl.ds(start, n), :].astype(F32)))

    pad_ref[pl.ds(lead, n_meta), :] = glu(n_real, n_meta)
    pad_ref[pl.ds(lead + n_meta, n_real), :] = glu(0, n_real)
    w = w_ref[...]
    bias = b_ref[...]

    def window(start, n):
        acc = jnp.broadcast_to(bias, (n, cb))
        for j in range(taps):
            acc = acc + w[j:j + 1] * pad_ref[pl.ds(start + j - half, n), :]
        return acc

    def body(ci, carry):
        base = pl.multiple_of(ci * chunk, chunk)
        o_ref[pl.ds(base, chunk), :] = window(base + lead + n_meta, chunk)
        return carry

    lax.fori_loop(0, n_real // chunk, body, 0)
    o_ref[pl.ds(n_real, n_meta), :] = window(lead, n_meta)


def _conformer_conv(proj3, col_blk, dw_w, dw_b, width, n_real, n_meta):
    b, l, _ = proj3.shape
    taps = dw_w.shape[0]
    cb = 128
    nc = width // cb
    lead = _round_up(taps // 2, 8)
    return pl.pallas_call(
        functools.partial(_conformer_kernel, n_real=n_real, n_meta=n_meta, taps=taps, chunk=64),
        grid=(b, nc),
        in_specs=[pl.BlockSpec((None, l, cb), lambda bi, c: (bi, 0, col_blk + c)),
                  pl.BlockSpec((None, l, cb), lambda bi, c: (bi, 0, col_blk + nc + c)),
                  pl.BlockSpec((taps, cb), lambda bi, c: (0, c)),
                  pl.BlockSpec((1, cb), lambda bi, c: (0, c))],
        out_specs=pl.BlockSpec((None, l, cb), lambda bi, c: (bi, 0, c)),
        out_shape=jax.ShapeDtypeStruct((b, l, width), F32),
        scratch_shapes=[pltpu.VMEM((l + 2 * lead, cb), F32)],
        compiler_params=_params(("arbitrary", "arbitrary")),
    )(proj3, proj3, dw_w, dw_b.reshape(1, width))


def _softmax_pv(parts, sink, out_dtype=F32):
    def rowwise(reduce_fn, combine, arrays):
        groups = {}
        for a in arrays:
            groups.setdefault(a.shape, []).append(a)
        return [reduce_fn(functools.reduce(combine, g), axis=-1, keepdims=True) for g in groups.values()]

    m = functools.reduce(jnp.maximum, rowwise(jnp.max, jnp.maximum, [s for s, _ in parts]), sink)
    es = [jnp.exp(s - m) for s, _ in parts]
    denom = functools.reduce(jnp.add, rowwise(jnp.sum, jnp.add, es), jnp.exp(sink - m))
    acc = None
    for e, (_, v) in zip(es, parts):
        pv = jnp.dot(e.astype(BF16), v, preferred_element_type=F32)
        acc = pv if acc is None else acc + pv
    return (acc / denom).astype(out_dtype)


def _qk(q, k):
    return lax.dot_general(q, k, (((1,), (1,)), ((), ())), preferred_element_type=F32)


def _attn_kernel(slope_ref, sink_ref, q_ref, kp_ref, kc_ref, kn_ref, km_ref, vp_ref, vc_ref, vn_ref, vm_ref,
                 o_ref, *, nb, n_meta, q_per_kv):
    g = pl.program_id(1)
    i = pl.program_id(2)
    t = ATT_BLOCK
    rows = q_per_kv * t
    row = lax.broadcasted_iota(jnp.int32, (rows, t), 0)
    a = row & (t - 1)
    c = lax.broadcasted_iota(jnp.int32, (rows, t), 1)
    hrow = lax.broadcasted_iota(jnp.int32, (rows, 1), 0) // t
    slope = jnp.zeros((rows, 1), F32)
    sink = jnp.zeros((rows, 1), F32)
    for hh in range(q_per_kv):
        slope = jnp.where(hrow == hh, slope_ref[g * q_per_kv + hh], slope)
        sink = jnp.where(hrow == hh, sink_ref[g * q_per_kv + hh], sink)
    scale = 1.0 / math.sqrt(HEAD_DIM)
    q = jnp.concatenate([q_ref[:, hh * HEAD_DIM:(hh + 1) * HEAD_DIM] for hh in range(q_per_kv)], axis=0)
    q = (q.astype(F32) * scale).astype(BF16)
    km = km_ref[...].astype(BF16)
    vm = vm_ref[...].astype(BF16)
    kn = kn_ref[...].astype(BF16)
    vn = vn_ref[...].astype(BF16)

    def store(o):
        o_ref[...] = jnp.concatenate([o[hh * t:(hh + 1) * t] for hh in range(q_per_kv)], axis=1)

    @pl.when(i < nb)
    def _():
        d_prev = (t + a - c).astype(F32)
        d_cur = jnp.abs(a - c).astype(F32)
        d_next = (t + c - a).astype(F32)
        ok_prev = c >= a + jnp.where(i > 0, 0, t)
        ok_next = c <= a - jnp.where(i < nb - 1, 0, t)
        kp, kc = kp_ref[...].astype(BF16), kc_ref[...].astype(BF16)
        vp, vc = vp_ref[...].astype(BF16), vc_ref[...].astype(BF16)
        s_m = _qk(q, km)
        s_p = jnp.where(ok_prev, _qk(q, kp) - slope * d_prev, NEG)
        s_c = _qk(q, kc) - slope * d_cur
        s_n = jnp.where(ok_next, _qk(q, kn) - slope * d_next, NEG)
        store(_softmax_pv([(s_m, vm), (s_p, vp), (s_c, vc), (s_n, vn)], sink))

    @pl.when(i == nb)
    def _():
        delta = n_meta + c - a
        s_m = _qk(q, km)
        s_0 = jnp.where(delta <= t, _qk(q, kn) - slope * delta.astype(F32), NEG)
        store(_softmax_pv([(s_m, vm), (s_0, vn)], sink))


def _windowed_attention(proj3, sink, q_col, k_col, v_col, n_q_heads, n_real, n_meta):
    b, l, _ = proj3.shape
    nb = n_real // ATT_BLOCK
    q_per_kv = n_q_heads // N_KV_HEADS
    qw = q_per_kv * HEAD_DIM
    qb, kb, vb = q_col // qw, k_col // HEAD_DIM, v_col // HEAD_DIM
    mb = n_real // n_meta
    slopes = jnp.asarray([2.0 ** (-8.0 * (h + 1) / n_q_heads) for h in range(n_q_heads)], F32)
    smem = pl.BlockSpec(memory_space=pltpu.SMEM)

    def kv_blk(colb, shift):
        def index(bi, g, i):
            blk = jnp.clip(i + shift, 0, nb - 1)
            if shift == 1:
                blk = jnp.where(i == nb, 0, blk)
            return bi, blk, colb + g
        return pl.BlockSpec((None, ATT_BLOCK, HEAD_DIM), index)

    kv_meta = lambda colb: pl.BlockSpec((None, n_meta, HEAD_DIM), lambda bi, g, i: (bi, mb, colb + g))
    return pl.pallas_call(
        functools.partial(_attn_kernel, nb=nb, n_meta=n_meta, q_per_kv=q_per_kv),
        grid=(b, N_KV_HEADS, nb + 1),
        in_specs=[smem, smem,
                  pl.BlockSpec((None, ATT_BLOCK, qw), lambda bi, g, i: (bi, i, qb + g)),
                  kv_blk(kb, -1), kv_blk(kb, 0), kv_blk(kb, 1), kv_meta(kb),
                  kv_blk(vb, -1), kv_blk(vb, 0), kv_blk(vb, 1), kv_meta(vb)],
        out_specs=pl.BlockSpec((None, ATT_BLOCK, qw), lambda bi, g, i: (bi, i, g)),
        out_shape=jax.ShapeDtypeStruct((b, l, n_q_heads * HEAD_DIM), F32),
        compiler_params=_params(("arbitrary", "arbitrary", "arbitrary")),
    )(slopes, sink, proj3, proj3, proj3, proj3, proj3, proj3, proj3, proj3, proj3)


def _group_norm_kernel(ya_ref, yb_ref, yc_ref, yd_ref, g_ref, lng_ref, lnb_ref, o_ref, *, widths):
    def rms(y, g):
        return y * lax.rsqrt(jnp.mean(y * y, axis=-1, keepdims=True) + EPS) * g

    off = 0
    for idx, (y_ref, w) in enumerate(zip((ya_ref, yb_ref, yc_ref, yd_ref), widths)):
        y = y_ref[...]
        if idx == 2:
            mu = jnp.mean(y, axis=-1, keepdims=True)
            yc = y - mu
            var = jnp.mean(yc * yc, axis=-1, keepdims=True)
            y = yc * lax.rsqrt(var + EPS) * lng_ref[...] + lnb_ref[...]
            y = y * jax.nn.sigmoid(y)
        o_ref[:, off:off + w] = rms(y, g_ref[:, off:off + w]).astype(o_ref.dtype)
        off += w


def _group_norms(ya, yb, yc, yd, group_g, ln_g, ln_b, tm=256):
    m = ya.shape[0]
    widths = (ya.shape[1], yb.shape[1], yc.shape[1], yd.shape[1])
    tot = sum(widths)
    row = lambda w: pl.BlockSpec((tm, w), lambda i: (i, 0))
    full = lambda w: pl.BlockSpec((1, w), lambda i: (0, 0))
    return pl.pallas_call(
        functools.partial(_group_norm_kernel, widths=widths),
        grid=(pl.cdiv(m, tm),),
        in_specs=[row(widths[0]), row(widths[1]), row(widths[2]), row(widths[3]),
                  full(tot), full(widths[2]), full(widths[2])],
        out_specs=row(tot),
        out_shape=jax.ShapeDtypeStruct((m, tot), BF16),
        compiler_params=_params(("arbitrary",)),
    )(ya, yb, yc, yd, group_g.reshape(1, tot), ln_g.reshape(1, -1), ln_b.reshape(1, -1))


def _norm_router_kernel(h_ref, g_ref, w_ref, o_ref):
    x = h_ref[...]
    ms = jnp.mean(x * x, axis=-1, keepdims=True)
    xn = (x * lax.rsqrt(ms + EPS) * g_ref[...]).astype(BF16)
    o_ref[...] = lax.dot_general(w_ref[...], xn, (((1,), (1,)), ((), ())), preferred_element_type=F32)


def _norm_router(h3, g, router_w, n_real, tm=256):
    b, _, d = h3.shape
    nt = n_real // tm
    wt = router_w.T.astype(BF16)
    return pl.pallas_call(
        _norm_router_kernel,
        grid=(b, nt),
        in_specs=[pl.BlockSpec((None, tm, d), lambda bi, i: (bi, i, 0)),
                  pl.BlockSpec((1, d), lambda bi, i: (0, 0)),
                  pl.BlockSpec((N_EXPERTS, d), lambda bi, i: (0, 0))],
        out_specs=pl.BlockSpec((N_EXPERTS, tm), lambda bi, i: (0, bi * nt + i)),
        out_shape=jax.ShapeDtypeStruct((N_EXPERTS, b * n_real), F32),
        compiler_params=_params(("arbitrary", "arbitrary")),
    )(h3, g.reshape(1, d), wt)


def _route_kernel(lg_ref, b_ref, slot1_ref, slot2_ref, g1_ref, g2_ref, te_ref, tv_ref, nx_ref, *, chunk):
    ne, m = lg_ref.shape
    logits = lg_ref[...] + b_ref[...]
    row = lax.broadcasted_iota(jnp.int32, (ne, m), 0)
    m1 = jnp.max(logits, axis=0, keepdims=True)
    e1 = jnp.min(jnp.where(logits == m1, row, ne), axis=0, keepdims=True)
    rest = jnp.where(row == e1, -jnp.inf, logits)
    m2 = jnp.max(rest, axis=0, keepdims=True)
    e2 = jnp.min(jnp.where(rest == m2, row, ne), axis=0, keepdims=True)
    g1 = 1.0 / (1.0 + jnp.exp(m2 - m1))
    g1_ref[...] = g1
    g2_ref[...] = 1.0 - g1
    oh1 = (row == e1).astype(F32)
    oh2 = (row == e2).astype(F32)
    c1 = jnp.sum(oh1, axis=1, keepdims=True)
    c2 = jnp.sum(oh2, axis=1, keepdims=True)
    padded = jnp.floor((c1 + c2 + (MOE_TILE - 1)) / MOE_TILE) * MOE_TILE
    erow = lax.broadcasted_iota(jnp.int32, (ne, 1), 0)
    start = jnp.zeros((ne, 1), F32)
    for e in range(ne - 1):
        start = start + jnp.where(erow > e, padded[e:e + 1, :], 0.0)
    end = start + padded
    tri = (lax.broadcasted_iota(jnp.int32, (chunk, chunk), 0)
           < lax.broadcasted_iota(jnp.int32, (chunk, chunk), 1)).astype(BF16)
    carry = jnp.concatenate([jnp.zeros((ne, 1), F32), c1], axis=0)
    base1 = start
    for ci in range(m // chunk):
        sl = slice(ci * chunk, (ci + 1) * chunk)
        oh = jnp.concatenate([oh1[:, sl], oh2[:, sl]], axis=0)
        pre = jnp.dot(oh.astype(BF16), tri, preferred_element_type=F32) + carry
        carry = carry + jnp.sum(oh, axis=1, keepdims=True)
        s1 = jnp.sum(oh[:ne] * (pre[:ne] + base1), axis=0, keepdims=True)
        s2 = jnp.sum(oh[ne:] * (pre[ne:] + base1), axis=0, keepdims=True)
        slot1_ref[:, sl] = s1.astype(jnp.int32)
        slot2_ref[:, sl] = s2.astype(jnp.int32)
    tile0 = (lax.broadcasted_iota(jnp.int32, (1, V7X_LANES), 1) * MOE_TILE).astype(F32)
    te = jnp.sum((end <= tile0).astype(jnp.int32), axis=0, keepdims=True)
    nonempty = padded > 0.0
    last = jnp.max(jnp.where(nonempty, erow, 0), axis=0, keepdims=True)
    te = jnp.minimum(te, last)
    te_ref[...] = te
    tv_ref[...] = (tile0 < end[ne - 1:ne, :]).astype(jnp.int32)
    nx_ref[...] = jnp.min(jnp.where(nonempty & (erow > te), erow, ne), axis=0, keepdims=True)


def _route(logits_t, router_b, n_tiles):
    ne, m = logits_t.shape
    assert n_tiles <= V7X_LANES
    vec = lambda dt: jax.ShapeDtypeStruct((1, m), dt)
    lane = jax.ShapeDtypeStruct((1, V7X_LANES), jnp.int32)
    return pl.pallas_call(
        functools.partial(_route_kernel, chunk=_pick_tile(m, (512, 256, 128))),
        out_shape=[vec(jnp.int32), vec(jnp.int32), vec(F32), vec(F32), lane, lane, lane],
        compiler_params=_params(None),
    )(logits_t, router_b.reshape(ne, 1))


def _row_copy(src_hbm, dst_vmem, sem, src_row, dst_row):
    return pltpu.make_async_copy(src_hbm.at[pl.ds(src_row, 1)], dst_vmem.at[pl.ds(dst_row, 1)], sem)


def _norm_scatter_kernel(s1_ref, s2_ref, h_ref, g_ref, xs_in_hbm, xs_hbm, pk_ref, sem, *, n_steps):
    del xs_in_hbm
    t = pl.program_id(0) * pl.num_programs(1) + pl.program_id(1)
    cur = t % 2
    tm = h_ref.shape[0]
    x = h_ref[...]
    ms = jnp.mean(x * x, axis=-1, keepdims=True)
    pk_ref[cur] = x * lax.rsqrt(ms + EPS) * g_ref[...]

    def row_out(buf, r, slot):
        return pltpu.make_async_copy(pk_ref.at[buf, pl.ds(r, 1)], xs_hbm.at[pl.ds(slot, 1)], sem.at[buf])

    def start(r4, carry):
        for k in range(ROW_DMA_UNROLL):
            r = r4 * ROW_DMA_UNROLL + k
            row_out(cur, r, s1_ref[0, 0, r]).start()
            row_out(cur, r, s2_ref[0, 0, r]).start()
        return carry

    lax.fori_loop(0, tm // ROW_DMA_UNROLL, start, 0)

    def wait_all(buf):
        def wait(r, carry):
            row_out(buf, r, 0).wait()
            row_out(buf, r, 0).wait()
            return carry

        lax.fori_loop(0, tm, wait, 0)

    @pl.when(t > 0)
    def _():
        wait_all(1 - cur)

    @pl.when(t == n_steps - 1)
    def _():
        wait_all(cur)


def _norm_scatter(h3, g, slot1, slot2, n_real, n_slots, tm=256):
    b, _, d = h3.shape
    nt = n_real // tm
    idx = lambda a: a.reshape(b * nt, 1, tm)
    smem_blk = pl.BlockSpec((1, 1, tm), lambda bi, i: (bi * nt + i, 0, 0), memory_space=pltpu.SMEM)
    return pl.pallas_call(
        functools.partial(_norm_scatter_kernel, n_steps=b * nt),
        grid=(b, nt),
        in_specs=[smem_blk, smem_blk,
                  pl.BlockSpec((None, tm, d), lambda bi, i: (bi, i, 0)),
                  pl.BlockSpec((1, d), lambda bi, i: (0, 0)),
                  pl.BlockSpec(memory_space=pl.ANY)],
        out_specs=pl.BlockSpec(memory_space=pl.ANY),
        out_shape=jax.ShapeDtypeStruct((n_slots, d), F32),
        scratch_shapes=[pltpu.VMEM((2, tm, d), F32), pltpu.SemaphoreType.DMA((2,))],
        input_output_aliases={4: 0},
        compiler_params=_params(("arbitrary", "arbitrary")),
    )(idx(slot1), idx(slot2), h3, g.reshape(1, d), jnp.zeros((n_slots, d), F32))


def _combine_kernel(s1_ref, s2_ref, n1_ref, n2_ref, h_ref, g1_ref, g2_ref, fg_ref, y_hbm, o_ref,
                    a_ref, b_ref, sem, *, n_steps):
    t = pl.program_id(0) * pl.num_programs(1) + pl.program_id(1)
    cur = t % 2
    n = a_ref.shape[1]

    def gather(buf, i1_ref, i2_ref):
        def start(r4, carry):
            for k in range(ROW_DMA_UNROLL):
                r = r4 * ROW_DMA_UNROLL + k
                _row_copy(y_hbm, a_ref.at[buf], sem.at[buf], i1_ref[0, 0, r], r).start()
                _row_copy(y_hbm, b_ref.at[buf], sem.at[buf], i2_ref[0, 0, r], r).start()
            return carry

        lax.fori_loop(0, n // ROW_DMA_UNROLL, start, 0)

    @pl.when(t == 0)
    def _():
        gather(0, s1_ref, s2_ref)

    @pl.when(t + 1 < n_steps)
    def _():
        gather(1 - cur, n1_ref, n2_ref)

    def wait(r, carry):
        _row_copy(y_hbm, a_ref.at[cur], sem.at[cur], 0, r).wait()
        _row_copy(y_hbm, b_ref.at[cur], sem.at[cur], 0, r).wait()
        return carry

    lax.fori_loop(0, n, wait, 0)
    h = h_ref[...] + g1_ref[...] * a_ref[cur] + g2_ref[...] * b_ref[cur]
    ms = jnp.mean(h * h, axis=-1, keepdims=True)
    o_ref[...] = h * lax.rsqrt(ms + EPS) * fg_ref[...]


def _moe_combine_final(h3, y, slot1, slot2, g1, g2, final_g, n_real, tm=256):
    b, _, d = h3.shape
    nt = n_real // tm
    idx = lambda a: a.reshape(b * nt, 1, tm)
    gate = lambda a: a.reshape(b, n_real, 1)
    smem_blk = pl.BlockSpec((1, 1, tm), lambda bi, i: (bi * nt + i, 0, 0), memory_space=pltpu.SMEM)
    smem_next = pl.BlockSpec((1, 1, tm), lambda bi, i: (jnp.minimum(bi * nt + i + 1, b * nt - 1), 0, 0),
                             memory_space=pltpu.SMEM)
    return pl.pallas_call(
        functools.partial(_combine_kernel, n_steps=b * nt),
        grid=(b, nt),
        in_specs=[smem_blk, smem_blk, smem_next, smem_next,
                  pl.BlockSpec((None, tm, d), lambda bi, i: (bi, i, 0)),
                  pl.BlockSpec((None, tm, 1), lambda bi, i: (bi, i, 0)),
                  pl.BlockSpec((None, tm, 1), lambda bi, i: (bi, i, 0)),
                  pl.BlockSpec((1, d), lambda bi, i: (0, 0)),
                  pl.BlockSpec(memory_space=pl.ANY)],
        out_specs=pl.BlockSpec((None, tm, d), lambda bi, i: (bi, i, 0)),
        out_shape=jax.ShapeDtypeStruct((b, n_real, d), F32),
        scratch_shapes=[pltpu.VMEM((2, tm, d), F32), pltpu.VMEM((2, tm, d), F32),
                        pltpu.SemaphoreType.DMA((2,))],
        compiler_params=_params(("arbitrary", "arbitrary")),
    )(idx(slot1), idx(slot2), idx(slot1), idx(slot2), h3, gate(g1), gate(g2), final_g.reshape(1, d), y)


def _filter_features(l, lhp, n_meta, n_emb):
    j = np.arange(lhp)
    r = np.concatenate([2 * j, 2 * j + 1])
    lp = 2 * lhp
    pos = (r + n_meta) % l
    valid = r < l
    t = pos / (l - 1.0)
    bands = (n_emb - 1) // 2
    w = 2.0 * np.pi * pos / l
    f = np.linspace(1e-4, bands - 1, bands)
    z = np.zeros((lp, V7X_LANES), np.float64)
    z[:, 0] = t
    z[:, 1:1 + bands] = np.cos(f[None, :] * w[:, None])
    z[:, 1 + bands:1 + 2 * bands] = -np.sin(f[None, :] * w[:, None])
    z *= valid[:, None]
    z[:, V7X_LANES - 1] = valid
    return jnp.asarray(z, F32)


def kernel(x, meta_tokens, norm_mix_g, w_in, hy_conv_w, hy_conv_b, hy_f_w1, hy_f_b1, hy_f_w2, hy_f_b2, hy_f_w3, hy_f_b3, hy_f_wo, hy_f_freq, hy_decay, hy_skip, cv_dw_w, cv_dw_b, cv_ln_g, cv_ln_b, attn_sink, group_norm_g, w_out, norm_ffn_g, ffn_w1, ffn_w3, ffn_w2, router_w, router_b, moe_w1, moe_w3, moe_w2, final_norm_g):
    b, n_real, d = x.shape
    n_meta = meta_tokens.shape[0]
    depth = w_in.shape[0]
    l = n_real + n_meta
    m = b * l
    w_hy = hy_skip.shape[1]
    w_fn = w_hy
    w_cv = cv_dw_b.shape[1]
    kv_w = N_KV_HEADS * HEAD_DIM
    p_in = w_in.shape[2]
    w_at = p_in - 3 * w_hy - w_fn - 2 * w_cv - 2 * kv_w
    n_q_heads = w_at // HEAD_DIM
    d_ff = ffn_w1.shape[2]
    assert depth == 2 and router_w.shape[2] == N_EXPERTS
    assert n_meta == V7X_BF16_ROWS and n_real % 256 == 0
    off_fn = 3 * w_hy
    off_cv = off_fn + w_fn
    off_q = off_cv + 2 * w_cv
    off_k = off_q + w_at
    off_v = off_k + kv_w

    lh = l // 2
    lhp = _round_up(lh + 1, V7X_LANES)
    j = np.arange(lhp)
    freq = np.arange(lhp)
    hy_unit, hy_period = np.pi / l, 2 * l
    tabs_f, tabs_t, tabs_n = [], [], []
    for par in (0, 1):
        col_pos = (2 * j + par + n_meta) % l
        tabs_f.append(_dft_tables(col_pos, j < lh, hy_unit, hy_period, 0, 1, lhp // 16 - 1,
                                  np.arange(lhp - 16, lhp), lhp))
        tabs_t.append(_dft_tables(freq, freq <= lh, hy_unit, hy_period, n_meta + par, 2, n_real // 32,
                                  np.arange(par, n_meta, 2), lhp))
        tabs_n.append(_dft_tables(col_pos, j < lh, 2 * np.pi / l, l, n_meta, 1, lh // 16,
                                  n_meta + 16 * (lh // 16) + np.arange(lh % 16), lhp))
    edge = np.where(freq == 0, 1.0, 2.0) / (2.0 * l)
    w1 = jnp.asarray(np.where(freq <= lh, edge, 0.0).reshape(lhp, 1), F32)
    w2 = jnp.asarray(np.where(freq < lh, edge, 0.0).reshape(lhp, 1), F32)
    zc = _filter_features(l, lhp, n_meta, hy_f_w1.shape[1])

    h = jnp.concatenate([x, jnp.broadcast_to(meta_tokens[None].astype(x.dtype), (b, n_meta, d))], axis=1)
    h = h.reshape(m, d)
    out = None
    for layer in range(depth):
        xn = _rmsnorm(h, norm_mix_g[layer], BF16)
        proj = _matmul(xn, [w_in], layer, tk=d, k_blk=0, n_out=p_in, tn=512, tm=MM_ROWS, out_dtype=BF16)
        proj3 = proj.reshape(b, l, p_in)

        hs, hd = _hyena_filter_sums(zc, hy_f_w1[layer], hy_f_b1[layer], hy_f_w2[layer], hy_f_b2[layer],
                                    hy_f_w3[layer], hy_f_b3[layer], hy_f_wo[layer], hy_f_freq[layer],
                                    hy_decay[layer], w_hy)
        kspec = _hyena_filter_spectrum(tabs_f[0], tabs_f[1], hs, hd, w1, w2)
        x0c, vxe, vxo = _hyena_prep(proj3, hy_conv_w[layer], hy_conv_b[layer], w_hy, n_real, n_meta, lhp)
        spec = _hyena_fwd(tabs_f[0], tabs_f[1], vxe, vxo, kspec)
        y_a = _hyena_inv(tabs_t[0], tabs_t[1], spec, vxe, vxo, x0c, hy_skip[layer])

        y_b = _fnet_time(tabs_n[0], tabs_n[1], _fnet_chan(proj3, off_fn // w_fn, w_fn, lhp), l)

        y_c = _conformer_conv(proj3, off_cv // 128, cv_dw_w[layer], cv_dw_b[layer], w_cv, n_real, n_meta)
        y_d = _windowed_attention(proj3, attn_sink[layer], off_q, off_k, off_v, n_q_heads, n_real, n_meta)

        ycat = _group_norms(y_a.reshape(m, w_hy), y_b.reshape(m, w_fn), y_c.reshape(m, w_cv),
                            y_d.reshape(m, w_at), group_norm_g[layer], cv_ln_g[layer], cv_ln_b[layer])
        h = _matmul(ycat, [w_out], layer, tk=ycat.shape[1], k_blk=0, n_out=d, tn=512, tm=MM_ROWS,
                    out_dtype=F32, residual=h)

        if layer % 2 == 0:
            i = layer // 2
            xn = _rmsnorm(h, norm_ffn_g[layer], BF16)
            tn_ff = _pick_tile(d_ff, (512, 256, 128))
            act = _matmul(xn, [ffn_w1, ffn_w3], i, tk=d, k_blk=0, n_out=d_ff, tn=tn_ff, tm=MM_ROWS,
                          out_dtype=BF16, swiglu=True)
            n_k = 2 if d_ff % (2 * V7X_LANES) == 0 and d_ff > 4096 else 1
            for kb in range(n_k):
                h = _matmul(act, [ffn_w2], i, tk=d_ff // n_k, k_blk=kb, n_out=d, tn=512, tm=1040,
                            out_dtype=F32, residual=h)
        else:
            i = layer // 2
            h3 = h.reshape(b, l, d)
            mr = b * n_real
            logits_t = _norm_router(h3, norm_ffn_g[layer], router_w[i], n_real)
            n_slots = _round_up(2 * mr + N_EXPERTS * (MOE_TILE - 1), MOE_TILE)
            n_tiles = n_slots // MOE_TILE
            slot1, slot2, g1, g2, te, tv, nx = _route(logits_t, router_b[i], n_tiles)
            slot1, slot2 = slot1.reshape(mr), slot2.reshape(mr)
            te, tv, nx = (a.reshape(-1)[:n_tiles] for a in (te, tv, nx))
            xs = _norm_scatter(h3, norm_ffn_g[layer], slot1, slot2, n_real, n_slots)
            act = _grouped_matmul(xs, [moe_w1[i], moe_w3[i]], te, tv, nx, tn=512, out_dtype=BF16, swiglu=True)
            ys = _grouped_matmul(act, [moe_w2[i]], te, tv, nx, tn=1024, out_dtype=F32)
            out = _moe_combine_final(h3, ys, slot1, slot2, g1, g2, final_norm_g, n_real)
    return out
```

```python
import functools
import math

import numpy as np
import jax
import jax.numpy as jnp
from jax import lax
from jax.experimental import pallas as pl
from jax.experimental.pallas import tpu as pltpu

F32 = jnp.float32
BF16 = jnp.bfloat16

V7X_LANES = 128
V7X_BF16_ROWS = 16
V7X_VMEM_LIMIT = 58 * 1024 * 1024

EPS = 1e-6
NEG = -1e30
HEAD_DIM = 128
N_KV_HEADS = 2
ATT_BLOCK = 128
FNET_GROUP_W = 256
N_EXPERTS = 8
MOE_TILE = 512
ROW_DMA_UNROLL = 4
PAD_FILL_ALIGN = 8
PAD_FILL_ROWS = MOE_TILE + PAD_FILL_ALIGN
MM_ROWS = 1376


def _round_up(x, m):
    return (x + m - 1) // m * m


def _pick_tile(n, candidates):
    for c in candidates:
        if n % c == 0:
            return c
    raise ValueError(f"no tile for {n} in {candidates}")


def _params(sem, vmem=None):
    if sem is None:
        return pltpu.CompilerParams(vmem_limit_bytes=vmem or V7X_VMEM_LIMIT)
    return pltpu.CompilerParams(dimension_semantics=sem, vmem_limit_bytes=vmem or V7X_VMEM_LIMIT)


def _rmsnorm_kernel(x_ref, g_ref, o_ref):
    x = x_ref[...]
    ms = jnp.mean(x * x, axis=-1, keepdims=True)
    o_ref[...] = (x * lax.rsqrt(ms + EPS) * g_ref[...]).astype(o_ref.dtype)


def _rmsnorm(x2d, g, out_dtype, tm=256):
    m, d = x2d.shape
    return pl.pallas_call(
        _rmsnorm_kernel,
        grid=(pl.cdiv(m, tm),),
        in_specs=[pl.BlockSpec((tm, d), lambda i: (i, 0)), pl.BlockSpec((1, d), lambda i: (0, 0))],
        out_specs=pl.BlockSpec((tm, d), lambda i: (i, 0)),
        out_shape=jax.ShapeDtypeStruct((m, d), out_dtype),
        compiler_params=_params(("arbitrary",)),
    )(x2d, g.reshape(1, d))


def _cast_tile(w_ref, wb_ref):
    k = w_ref.shape[0]
    ck = _pick_tile(k, (256, 128, 64, 32, 16))

    def body(c, carry):
        r = pl.multiple_of(c * ck, ck)
        wb_ref[pl.ds(r, ck), :] = w_ref[pl.ds(r, ck), :].astype(BF16)
        return carry

    lax.fori_loop(0, k // ck, body, 0)


def _mm_kernel(x_ref, *refs, n_w, swiglu, has_res, layer, k_off, tn):
    w_hbm = refs[:n_w]
    res_ref = refs[n_w] if has_res else None
    o_ref, stage_ref, wb_ref, sem = refs[n_w + int(has_res):n_w + int(has_res) + 4]
    j = pl.program_id(0)
    i = pl.program_id(1)
    tk = stage_ref.shape[1]

    def w_copy(idx, col_tile):
        col = pl.multiple_of(col_tile * tn, tn)
        return pltpu.make_async_copy(w_hbm[idx].at[layer, pl.ds(k_off, tk), pl.ds(col, tn)],
                                     stage_ref.at[idx], sem.at[idx])

    @pl.when((j == 0) & (i == 0))
    def _():
        for idx in range(n_w):
            w_copy(idx, 0).start()

    @pl.when(i == 0)
    def _():
        for idx in range(n_w):
            w_copy(idx, j).wait()
            _cast_tile(stage_ref.at[idx], wb_ref.at[idx])

        @pl.when(j + 1 < pl.num_programs(0))
        def _():
            for idx in range(n_w):
                w_copy(idx, j + 1).start()

    x = x_ref[...]
    acc = jnp.dot(x, wb_ref[0], preferred_element_type=F32)
    if swiglu:
        acc = acc * jax.nn.sigmoid(acc) * jnp.dot(x, wb_ref[1], preferred_element_type=F32)
    if has_res:
        acc = acc + res_ref[...]
    o_ref[...] = acc.astype(o_ref.dtype)


def _matmul(x, ws, layer, *, tk, k_blk, n_out, tn, tm, out_dtype, residual=None, swiglu=False):
    m = x.shape[0]
    n_w = len(ws)
    tm = min(tm, _round_up(m, V7X_BF16_ROWS))
    in_specs = [pl.BlockSpec((tm, tk), lambda j, i: (i, k_blk))]
    in_specs += [pl.BlockSpec(memory_space=pl.ANY) for _ in ws]
    args = [x, *ws]
    if residual is not None:
        in_specs.append(pl.BlockSpec((tm, tn), lambda j, i: (i, j)))
        args.append(residual)
    kern = functools.partial(_mm_kernel, n_w=n_w, swiglu=swiglu, has_res=residual is not None,
                             layer=layer, k_off=k_blk * tk, tn=tn)
    return pl.pallas_call(
        kern,
        grid=(n_out // tn, pl.cdiv(m, tm)),
        in_specs=in_specs,
        out_specs=pl.BlockSpec((tm, tn), lambda j, i: (i, j)),
        out_shape=jax.ShapeDtypeStruct((m, n_out), out_dtype),
        scratch_shapes=[pltpu.VMEM((n_w, tk, tn), F32), pltpu.VMEM((n_w, tk, tn), BF16),
                        pltpu.SemaphoreType.DMA((n_w,))],
        compiler_params=_params(("arbitrary", "arbitrary")),
    )(*args)


def _gmm_kernel(te_ref, tv_ref, nx_ref, x_ref, *refs, n_w, swiglu, tn):
    w_hbm = refs[:n_w]
    o_ref, stage_ref, wb_ref, sem = refs[n_w:n_w + 4]
    j = pl.program_id(0)
    t = pl.program_id(1)
    n_j = pl.num_programs(0)
    e = te_ref[t]

    def w_copy(i, expert, col_tile):
        col = pl.multiple_of(col_tile * tn, tn)
        return pltpu.make_async_copy(w_hbm[i].at[expert, :, pl.ds(col, tn)], stage_ref.at[i], sem.at[i])

    @pl.when((j == 0) & (t == 0))
    def _():
        for i in range(n_w):
            w_copy(i, e, 0).start()

    @pl.when((t == 0) | (e != te_ref[jnp.maximum(t - 1, 0)]))
    def _():
        for i in range(n_w):
            w_copy(i, e, j).wait()
            _cast_tile(stage_ref.at[i], wb_ref.at[i])
        e_next = nx_ref[t]

        @pl.when(e_next < N_EXPERTS)
        def _():
            for i in range(n_w):
                w_copy(i, e_next, j).start()

        @pl.when((e_next >= N_EXPERTS) & (j + 1 < n_j))
        def _():
            for i in range(n_w):
                w_copy(i, te_ref[0], j + 1).start()

    @pl.when(tv_ref[t] != 0)
    def _():
        x = x_ref[...].astype(BF16)
        acc = jnp.dot(x, wb_ref[0], preferred_element_type=F32)
        if swiglu:
            acc = acc * jax.nn.sigmoid(acc) * jnp.dot(x, wb_ref[1], preferred_element_type=F32)
        o_ref[...] = acc.astype(o_ref.dtype)

    @pl.when(tv_ref[t] == 0)
    def _():
        o_ref[...] = jnp.zeros_like(o_ref)


def _grouped_matmul(x, ws, tile_expert, tile_valid, next_expert, *, tn, out_dtype, swiglu=False):
    s, k = x.shape
    n = ws[0].shape[2]
    n_w = len(ws)
    grid_spec = pltpu.PrefetchScalarGridSpec(
        num_scalar_prefetch=3,
        grid=(n // tn, s // MOE_TILE),
        in_specs=[pl.BlockSpec((MOE_TILE, k), lambda j, t, te, tv, nx: (t, 0))]
        + [pl.BlockSpec(memory_space=pl.ANY) for _ in ws],
        out_specs=pl.BlockSpec((MOE_TILE, tn), lambda j, t, te, tv, nx: (t, j)),
        scratch_shapes=[pltpu.VMEM((n_w, k, tn), F32), pltpu.VMEM((n_w, k, tn), BF16),
                        pltpu.SemaphoreType.DMA((n_w,))],
    )
    return pl.pallas_call(
        functools.partial(_gmm_kernel, n_w=n_w, swiglu=swiglu, tn=tn),
        grid_spec=grid_spec,
        out_shape=jax.ShapeDtypeStruct((s, n), out_dtype),
        compiler_params=_params(("arbitrary", "arbitrary")),
    )(tile_expert, tile_valid, next_expert, x, *ws)


def _dft_table_kernel(ac_ref, as_ref, bc_ref, bs_ref, cc_ref, cs_ref, tc_ref, ts_ref, c_ref, s_ref, *,
                      n_blocks, n_rows_out):
    ac = ac_ref[...]
    asn = as_ref[...]

    def body(q, carry):
        v = q % 16
        u = q // 16
        bc, bs = bc_ref[pl.ds(v, 1), :], bs_ref[pl.ds(v, 1), :]
        cc, cs = cc_ref[pl.ds(u, 1), :], cs_ref[pl.ds(u, 1), :]
        dc = bc * cc - bs * cs
        dsn = bs * cc + bc * cs
        row = pl.multiple_of(q * 16, 16)
        c_ref[pl.ds(row, 16), :] = (ac * dc - asn * dsn).astype(BF16)
        s_ref[pl.ds(row, 16), :] = (asn * dc + ac * dsn).astype(BF16)
        return carry

    lax.fori_loop(0, n_blocks, body, 0)
    c_ref[pl.ds(16 * n_blocks, 16), :] = tc_ref[...].astype(BF16)
    s_ref[pl.ds(16 * n_blocks, 16), :] = ts_ref[...].astype(BF16)
    pad = n_rows_out - 16 * (n_blocks + 1)
    if pad:
        c_ref[pl.ds(16 * (n_blocks + 1), pad), :] = jnp.zeros((pad, c_ref.shape[1]), BF16)
        s_ref[pl.ds(16 * (n_blocks + 1), pad), :] = jnp.zeros((pad, s_ref.shape[1]), BF16)


def _dft_tables(col_pos, col_valid, unit, period, row0, row_step, n_blocks, tail_pos, n_rows_out):
    ncol = col_pos.shape[0]
    n_u = -(-n_blocks // 16)
    p = col_pos.astype(np.int64)[None, :]
    valid = col_valid[None, :].astype(np.float64)
    assert len(tail_pos) <= 16 and n_rows_out >= 16 * (n_blocks + 1)

    def factor(rowpos, mask):
        ang = unit * ((np.asarray(rowpos, dtype=np.int64)[:, None] * p) % period)
        return (jnp.asarray((np.cos(ang) * mask).astype(np.float32)),
                jnp.asarray((np.sin(ang) * mask).astype(np.float32)))

    a_c, a_s = factor(row0 + row_step * np.arange(16), valid)
    b_c, b_s = factor(row_step * 16 * np.arange(16), 1.0)
    c_c, c_s = factor(row_step * 256 * np.arange(n_u), 1.0)
    tail = np.zeros(16, np.int64)
    tail[:len(tail_pos)] = tail_pos
    t_c, t_s = factor(tail, valid * (np.arange(16) < len(tail_pos))[:, None])
    pc = _pick_tile(ncol, (2176, 1408, 384, 256, 128))
    kern = functools.partial(_dft_table_kernel, n_blocks=n_blocks, n_rows_out=n_rows_out)
    col = lambda rows: pl.BlockSpec((rows, pc), lambda j: (0, j))
    return pl.pallas_call(
        kern,
        grid=(ncol // pc,),
        in_specs=[col(16), col(16), col(16), col(16), col(n_u), col(n_u), col(16), col(16)],
        out_specs=[col(n_rows_out), col(n_rows_out)],
        out_shape=[jax.ShapeDtypeStruct((n_rows_out, ncol), BF16)] * 2,
        compiler_params=_params(("arbitrary",)),
    )(a_c, a_s, b_c, b_s, c_c, c_s, t_c, t_s)


def _filter_kernel(z_ref, w1_ref, b1_ref, w2_ref, b2_ref, w3_ref, b3_ref, wo_ref, fr_ref, dec_ref,
                   hs_ref, hd_ref, *, width):
    hp = lax.Precision.HIGHEST
    z = z_ref[...]
    fr = fr_ref[...]
    hid = jnp.sin(fr * (jnp.dot(z, w1_ref[...], precision=hp, preferred_element_type=F32) + b1_ref[...]))
    hid = jnp.sin(fr * (jnp.dot(hid, w2_ref[...], precision=hp, preferred_element_type=F32) + b2_ref[...]))
    hid = jnp.sin(fr * (jnp.dot(hid, w3_ref[...], precision=hp, preferred_element_type=F32) + b3_ref[...]))
    h = jnp.dot(hid, wo_ref[...], precision=hp, preferred_element_type=F32)
    t = z[:, 0:1]
    valid = z[:, V7X_LANES - 1:V7X_LANES]
    h = h * jnp.exp(-t * dec_ref[...]) * valid
    hf = h[:, :width]
    hb = h[:, width:]
    hs_ref[...] = (hf + hb).astype(BF16)
    hd_ref[...] = (hf - hb).astype(BF16)


def _hyena_filter_sums(zc, w1, b1, w2, b2, w3, b3, wo, freq, decay, width):
    lp = zc.shape[0]
    hid = w2.shape[0]
    tr = _pick_tile(lp, (528, 384, 256, 128))
    w1p = jnp.zeros((V7X_LANES, hid), F32).at[:w1.shape[0]].set(w1)
    full = lambda shape: pl.BlockSpec(shape, lambda i: (0, 0))
    return pl.pallas_call(
        functools.partial(_filter_kernel, width=width),
        grid=(lp // tr,),
        in_specs=[pl.BlockSpec((tr, V7X_LANES), lambda i: (i, 0)),
                  full((V7X_LANES, hid)), full((1, hid)), full((hid, hid)), full((1, hid)),
                  full((hid, hid)), full((1, hid)), full((hid, 2 * width)), full((1, hid)),
                  full((1, 2 * width))],
        out_specs=[pl.BlockSpec((tr, width), lambda i: (i, 0))] * 2,
        out_shape=[jax.ShapeDtypeStruct((lp, width), BF16)] * 2,
        compiler_params=_params(("arbitrary",)),
    )(zc, w1p, b1.reshape(1, hid), w2, b2.reshape(1, hid), w3, b3.reshape(1, hid), wo,
      freq.reshape(1, hid), decay.reshape(1, 2 * width))


def _kf_kernel(ce_ref, se_ref, co_ref, so_ref, hse_ref, hso_ref, hde_ref, hdo_ref, w1_ref, w2_ref,
               k1re_ref, k1im_ref, k2re_ref, k2im_ref):
    ka = jnp.dot(ce_ref[...], hse_ref[...], preferred_element_type=F32)
    kb = jnp.dot(co_ref[...], hso_ref[...], preferred_element_type=F32)
    sa = jnp.dot(se_ref[...], hde_ref[...], preferred_element_type=F32)
    sb = jnp.dot(so_ref[...], hdo_ref[...], preferred_element_type=F32)
    w1 = w1_ref[...]
    w2 = w2_ref[...]
    k1re_ref[...] = w1 * (ka + kb)
    k1im_ref[...] = -w1 * (sa + sb)
    k2re_ref[...] = w2 * (ka - kb)
    k2im_ref[...] = w2 * (sa - sb)


def _half_tile(n):
    return _pick_tile(n, (544, 384, 256, 128))


def _hyena_filter_spectrum(tabs_e, tabs_o, hs, hd, w1, w2):
    (ce, se), (co, so) = tabs_e, tabs_o
    f2p, lhp = ce.shape
    width = hs.shape[1]
    tf = _half_tile(f2p)
    cb = 512
    tab = pl.BlockSpec((tf, lhp), lambda c, f: (f, 0))
    half = lambda which: pl.BlockSpec((lhp, cb), lambda c, f: (which, c))
    wcol = pl.BlockSpec((tf, 1), lambda c, f: (f, 0))
    return pl.pallas_call(
        _kf_kernel,
        grid=(width // cb, f2p // tf),
        in_specs=[tab, tab, tab, tab, half(0), half(1), half(0), half(1), wcol, wcol],
        out_specs=[pl.BlockSpec((tf, cb), lambda c, f: (f, c))] * 4,
        out_shape=[jax.ShapeDtypeStruct((f2p, width), F32)] * 4,
        compiler_params=_params(("arbitrary", "arbitrary")),
    )(ce, se, co, so, hs, hs, hd, hd, w1, w2)


def _hyena_prep_kernel(x0_ref, x1_ref, v_ref, w0_ref, w1_ref, wv_ref, b0_ref, b1_ref, bv_ref,
                       x0c_ref, vxe_ref, vxo_ref, pad_ref, vx_ref, *, n_real, n_meta):
    cb = pad_ref.shape[1]
    l = n_real + n_meta
    pad_ref[pl.ds(0, 8), :] = jnp.zeros((8, cb), F32)
    pad_ref[pl.ds(8 + l, 8), :] = jnp.zeros((8, cb), F32)

    def conv(u_ref, w_ref, b_ref):
        pad_ref[pl.ds(8, n_meta), :] = u_ref[pl.ds(n_real, n_meta), :].astype(F32)
        pad_ref[pl.ds(8 + n_meta, n_real), :] = u_ref[pl.ds(0, n_real), :].astype(F32)
        w = w_ref[...]
        b = b_ref[...]

        def at(start, n):
            return (w[0:1] * pad_ref[pl.ds(start - 1, n), :] + w[1:2] * pad_ref[pl.ds(start, n), :]
                    + w[2:3] * pad_ref[pl.ds(start + 1, n), :] + b)

        return at(8 + n_meta, n_real), at(8, n_meta)

    x0_real, x0_meta = conv(x0_ref, w0_ref, b0_ref)
    x0c_ref[pl.ds(0, n_real), :] = x0_real
    x0c_ref[pl.ds(n_real, n_meta), :] = x0_meta
    x1_real, x1_meta = conv(x1_ref, w1_ref, b1_ref)
    v_real, v_meta = conv(v_ref, wv_ref, bv_ref)
    vx_ref[pl.ds(0, n_real), :] = v_real * x1_real
    vx_ref[pl.ds(n_real, n_meta), :] = v_meta * x1_meta
    lh = l // 2
    lhp = vxe_ref.shape[0]
    for par, o_ref in ((0, vxe_ref), (1, vxo_ref)):
        o_ref[pl.ds(0, lh), :] = vx_ref[pl.ds(par, lh, stride=2), :].astype(BF16)
        o_ref[pl.ds(lh, lhp - lh), :] = jnp.zeros((lhp - lh, cb), BF16)


def _hyena_prep(proj3, conv_w, conv_b, width, n_real, n_meta, lhp):
    b, l, _ = proj3.shape
    cb = 128
    nc = width // cb
    col = lambda off: pl.BlockSpec((None, l, cb), lambda bi, c: (bi, 0, off + c))
    wcol = lambda off: pl.BlockSpec((3, cb), lambda bi, c: (0, off + c))
    bcol = lambda off: pl.BlockSpec((1, cb), lambda bi, c: (0, off + c))
    cbias = conv_b.reshape(1, 3 * width)
    half = pl.BlockSpec((None, lhp, cb), lambda bi, c: (bi, 0, c))
    return pl.pallas_call(
        functools.partial(_hyena_prep_kernel, n_real=n_real, n_meta=n_meta),
        grid=(b, nc),
        in_specs=[col(0), col(nc), col(2 * nc), wcol(0), wcol(nc), wcol(2 * nc),
                  bcol(0), bcol(nc), bcol(2 * nc)],
        out_specs=[pl.BlockSpec((None, l, cb), lambda bi, c: (bi, 0, c)), half, half],
        out_shape=[jax.ShapeDtypeStruct((b, l, width), F32)]
        + [jax.ShapeDtypeStruct((b, lhp, width), BF16)] * 2,
        scratch_shapes=[pltpu.VMEM((l + 16, cb), F32), pltpu.VMEM((l, cb), F32)],
        compiler_params=_params(("arbitrary", "arbitrary")),
    )(proj3, proj3, proj3, conv_w, conv_w, conv_w, cbias, cbias, cbias)


def _hyena_fwd_kernel(ce_ref, se_ref, co_ref, so_ref, vxe_ref, vxo_ref, k1re_ref, k1im_ref, k2re_ref, k2im_ref,
                      ere_ref, eim_ref, ore_ref, oim_ref):
    vxe = vxe_ref[...]
    vxo = vxo_ref[...]
    a_re = jnp.dot(ce_ref[...], vxe, preferred_element_type=F32)
    b_re = jnp.dot(co_ref[...], vxo, preferred_element_type=F32)
    a_im = jnp.dot(se_ref[...], vxe, preferred_element_type=F32)
    b_im = jnp.dot(so_ref[...], vxo, preferred_element_type=F32)
    u1re, u1im = a_re + b_re, a_im + b_im
    u2re, u2im = a_re - b_re, b_im - a_im
    k1re, k1im, k2re, k2im = k1re_ref[...], k1im_ref[...], k2re_ref[...], k2im_ref[...]
    p_re = u1re * k1re + u1im * k1im
    p_im = u1re * k1im - u1im * k1re
    q_re = u2re * k2re + u2im * k2im
    q_im = u2re * k2im - u2im * k2re
    ere_ref[...] = (p_re + q_re).astype(BF16)
    eim_ref[...] = (p_im - q_im).astype(BF16)
    ore_ref[...] = (p_re - q_re).astype(BF16)
    oim_ref[...] = (p_im + q_im).astype(BF16)


def _hyena_fwd(tabs_e, tabs_o, vxe, vxo, kspec):
    (ce, se), (co, so) = tabs_e, tabs_o
    f2p, lhp = ce.shape
    b, _, width = vxe.shape
    tf = _half_tile(f2p)
    cb = 512
    tab = pl.BlockSpec((tf, lhp), lambda c, bi, f: (f, 0))
    vx_blk = pl.BlockSpec((None, lhp, cb), lambda c, bi, f: (bi, 0, c))
    k_blk = pl.BlockSpec((tf, cb), lambda c, bi, f: (f, c))
    return pl.pallas_call(
        _hyena_fwd_kernel,
        grid=(width // cb, b, f2p // tf),
        in_specs=[tab, tab, tab, tab, vx_blk, vx_blk, k_blk, k_blk, k_blk, k_blk],
        out_specs=[pl.BlockSpec((None, tf, cb), lambda c, bi, f: (bi, f, c))] * 4,
        out_shape=[jax.ShapeDtypeStruct((b, f2p, width), BF16)] * 4,
        compiler_params=_params(("arbitrary", "arbitrary", "arbitrary")),
    )(ce, se, co, so, vxe, vxo, *kspec)


def _hyena_inv_kernel(cet_ref, set_ref, cot_ref, sot_ref, ere_ref, eim_ref, ore_ref, oim_ref,
                      vxe_ref, vxo_ref, x0c_ref, skip_ref, o_ref, mix_ref):
    tr = vxe_ref.shape[0]
    skip = skip_ref[...]
    ye = (jnp.dot(cet_ref[...], ere_ref[...], preferred_element_type=F32)
          - jnp.dot(set_ref[...], eim_ref[...], preferred_element_type=F32))
    yo = (jnp.dot(cot_ref[...], ore_ref[...], preferred_element_type=F32)
          - jnp.dot(sot_ref[...], oim_ref[...], preferred_element_type=F32))
    te = ye + skip * vxe_ref[...].astype(F32)
    to = yo + skip * vxo_ref[...].astype(F32)
    for c in range(o_ref.shape[1] // V7X_LANES):
        lanes = slice(c * V7X_LANES, (c + 1) * V7X_LANES)
        mix_ref[c, pl.ds(0, tr, stride=2), :] = te[:, lanes]
        mix_ref[c, pl.ds(1, tr, stride=2), :] = to[:, lanes]
        o_ref[:, lanes] = mix_ref[c] * x0c_ref[:, lanes]


def _hyena_inv(tabs_et, tabs_ot, spec, vxe, vxo, x0c, skip):
    (cet, set_), (cot, sot) = tabs_et, tabs_ot
    lhp, f2p = cet.shape
    b, l, width = x0c.shape
    tr = _half_tile(lhp)
    cb = 512
    tab = pl.BlockSpec((tr, f2p), lambda c, bi, r: (r, 0))
    s_blk = pl.BlockSpec((None, f2p, cb), lambda c, bi, r: (bi, 0, c))
    vx_blk = pl.BlockSpec((None, tr, cb), lambda c, bi, r: (bi, r, c))
    row_blk = pl.BlockSpec((None, 2 * tr, cb), lambda c, bi, r: (bi, r, c))
    return pl.pallas_call(
        _hyena_inv_kernel,
        grid=(width // cb, b, lhp // tr),
        in_specs=[tab, tab, tab, tab, s_blk, s_blk, s_blk, s_blk, vx_blk, vx_blk, row_blk,
                  pl.BlockSpec((1, cb), lambda c, bi, r: (0, c))],
        out_specs=row_blk,
        out_shape=jax.ShapeDtypeStruct((b, l, width), F32),
        scratch_shapes=[pltpu.VMEM((cb // V7X_LANES, 2 * tr, V7X_LANES), F32)],
        compiler_params=_params(("arbitrary", "arbitrary", "arbitrary")),
    )(cet, set_, cot, sot, *spec, vxe, vxo, x0c, skip.reshape(1, width))


def _fnet_chan_kernel(u_ref, cc_ref, sc_ref, ae_ref, be_ref, ao_ref, bo_ref, *, l, tr):
    row = pl.program_id(1) * tr + lax.broadcasted_iota(jnp.int32, (tr, 1), 0)
    u = jnp.where(row < l, u_ref[...], 0.0).astype(BF16)
    cc = cc_ref[...].astype(BF16)
    sc = sc_ref[...].astype(BF16)
    pick = lax.broadcasted_iota(jnp.int32, (tr // 2, tr), 1) - 2 * lax.broadcasted_iota(jnp.int32, (tr // 2, tr), 0)
    gw = FNET_GROUP_W
    for par, a_ref, b_ref in ((0, ae_ref, be_ref), (1, ao_ref, bo_ref)):
        sel = jnp.where(pick == par, 1.0, 0.0).astype(BF16)
        up = jnp.dot(sel, u, preferred_element_type=F32).astype(BF16)
        for g in range(u.shape[1] // gw):
            ug = up[:, g * gw:(g + 1) * gw]
            a_ref[:, g * gw:(g + 1) * gw] = jnp.dot(ug, cc, preferred_element_type=F32).astype(BF16)
            b_ref[:, g * gw:(g + 1) * gw] = jnp.dot(ug, sc, preferred_element_type=F32).astype(BF16)


def _fnet_chan(proj3, col_blk, width, lhp):
    b, l, _ = proj3.shape
    tr = 256
    k = np.arange(FNET_GROUP_W)
    ang = 2.0 * np.pi * ((k[:, None] * k[None, :]) % FNET_GROUP_W) / FNET_GROUP_W
    cc = jnp.asarray(np.cos(ang), F32)
    sc = jnp.asarray(np.sin(ang), F32)
    return pl.pallas_call(
        functools.partial(_fnet_chan_kernel, l=l, tr=tr),
        grid=(b, 2 * lhp // tr),
        in_specs=[pl.BlockSpec((None, tr, width), lambda bi, r: (bi, r, col_blk)),
                  pl.BlockSpec((FNET_GROUP_W, FNET_GROUP_W), lambda bi, r: (0, 0)),
                  pl.BlockSpec((FNET_GROUP_W, FNET_GROUP_W), lambda bi, r: (0, 0))],
        out_specs=[pl.BlockSpec((None, tr // 2, width), lambda bi, r: (bi, r, 0))] * 4,
        out_shape=[jax.ShapeDtypeStruct((b, lhp, width), BF16)] * 4,
        compiler_params=_params(("arbitrary", "arbitrary")),
    )(proj3, cc, sc)


def _fnet_time_kernel(ce_ref, se_ref, co_ref, so_ref, ae_ref, be_ref, ao_ref, bo_ref, o_ref, *, scale):
    p = (jnp.dot(ce_ref[...], ae_ref[...], preferred_element_type=F32)
         - jnp.dot(se_ref[...], be_ref[...], preferred_element_type=F32))
    q = (jnp.dot(co_ref[...], ao_ref[...], preferred_element_type=F32)
         - jnp.dot(so_ref[...], bo_ref[...], preferred_element_type=F32))
    o_ref[0] = (p + q) * scale
    o_ref[1] = (p - q) * scale


def _fnet_time(tabs_e, tabs_o, parts, l):
    (ce, se), (co, so) = tabs_e, tabs_o
    lhp = ce.shape[0]
    b, _, width = parts[0].shape
    tr = _half_tile(lhp)
    cb = 512
    scale = 1.0 / math.sqrt(l * FNET_GROUP_W)
    tab = pl.BlockSpec((tr, lhp), lambda c, bi, r: (r, 0))
    blk = pl.BlockSpec((None, lhp, cb), lambda c, bi, r: (bi, 0, c))
    return pl.pallas_call(
        functools.partial(_fnet_time_kernel, scale=scale),
        grid=(width // cb, b, lhp // tr),
        in_specs=[tab, tab, tab, tab, blk, blk, blk, blk],
        out_specs=pl.BlockSpec((None, 2, tr, cb), lambda c, bi, r: (bi, 0, r, c)),
        out_shape=jax.ShapeDtypeStruct((b, 2, l // 2, width), F32),
        compiler_params=_params(("arbitrary", "arbitrary", "arbitrary")),
    )(ce, se, co, so, *parts)


def _conformer_kernel(a_ref, g_ref, w_ref, b_ref, o_ref, pad_ref, *, n_real, n_meta, taps, chunk):
    cb = pad_ref.shape[1]
    half = taps // 2
    lead = _round_up(half, 8)
    l = n_real + n_meta
    pad_ref[pl.ds(0, lead), :] = jnp.zeros((lead, cb), F32)
    pad_ref[pl.ds(lead + l, lead), :] = jnp.zeros((lead, cb), F32)
    def glu(start, n):
        return (a_ref[pl.ds(start, n), :].astype(F32)
                * jax.nn.sigmoid(g_ref[pl.ds(start, n), :].astype(F32)))

    pad_ref[pl.ds(lead, n_meta), :] = glu(n_real, n_meta)
    pad_ref[pl.ds(lead + n_meta, n_real), :] = glu(0, n_real)
    w = w_ref[...]
    bias = b_ref[...]

    def window(start, n):
        acc = jnp.broadcast_to(bias, (n, cb))
        for j in range(taps):
            acc = acc + w[j:j + 1] * pad_ref[pl.ds(start + j - half, n), :]
        return acc

    def body(ci, carry):
        base = pl.multiple_of(ci * chunk, chunk)
        o_ref[pl.ds(base, chunk), :] = window(base + lead + n_meta, chunk)
        return carry

    lax.fori_loop(0, n_real // chunk, body, 0)
    o_ref[pl.ds(n_real, n_meta), :] = window(lead, n_meta)


def _conformer_conv(proj3, col_blk, dw_w, dw_b, width, n_real, n_meta):
    b, l, _ = proj3.shape
    taps = dw_w.shape[0]
    cb = 128
    nc = width // cb
    lead = _round_up(taps // 2, 8)
    return pl.pallas_call(
        functools.partial(_conformer_kernel, n_real=n_real, n_meta=n_meta, taps=taps, chunk=64),
        grid=(b, nc),
        in_specs=[pl.BlockSpec((None, l, cb), lambda bi, c: (bi, 0, col_blk + c)),
                  pl.BlockSpec((None, l, cb), lambda bi, c: (bi, 0, col_blk + nc + c)),
                  pl.BlockSpec((taps, cb), lambda bi, c: (0, c)),
                  pl.BlockSpec((1, cb), lambda bi, c: (0, c))],
        out_specs=pl.BlockSpec((None, l, cb), lambda bi, c: (bi, 0, c)),
        out_shape=jax.ShapeDtypeStruct((b, l, width), F32),
        scratch_shapes=[pltpu.VMEM((l + 2 * lead, cb), F32)],
        compiler_params=_params(("arbitrary", "arbitrary")),
    )(proj3, proj3, dw_w, dw_b.reshape(1, width))


def _softmax_pv(parts, sink, out_dtype=F32):
    def rowwise(reduce_fn, combine, arrays):
        groups = {}
        for a in arrays:
            groups.setdefault(a.shape, []).append(a)
        return [reduce_fn(functools.reduce(combine, g), axis=-1, keepdims=True) for g in groups.values()]

    m = functools.reduce(jnp.maximum, rowwise(jnp.max, jnp.maximum, [s for s, _ in parts]), sink)
    es = [jnp.exp(s - m) for s, _ in parts]
    denom = functools.reduce(jnp.add, rowwise(jnp.sum, jnp.add, es), jnp.exp(sink - m))
    acc = None
    for e, (_, v) in zip(es, parts):
        pv = jnp.dot(e.astype(BF16), v, preferred_element_type=F32)
        acc = pv if acc is None else acc + pv
    return (acc / denom).astype(out_dtype)


def _qk(q, k):
    return lax.dot_general(q, k, (((1,), (1,)), ((), ())), preferred_element_type=F32)


def _attn_kernel(slope_ref, sink_ref, q_ref, kp_ref, kc_ref, kn_ref, km_ref, vp_ref, vc_ref, vn_ref, vm_ref,
                 o_ref, *, nb, n_meta, q_per_kv):
    g = pl.program_id(1)
    i = pl.program_id(2)
    t = ATT_BLOCK
    rows = q_per_kv * t
    row = lax.broadcasted_iota(jnp.int32, (rows, t), 0)
    a = row & (t - 1)
    c = lax.broadcasted_iota(jnp.int32, (rows, t), 1)
    hrow = lax.broadcasted_iota(jnp.int32, (rows, 1), 0) // t
    slope = jnp.zeros((rows, 1), F32)
    sink = jnp.zeros((rows, 1), F32)
    for hh in range(q_per_kv):
        slope = jnp.where(hrow == hh, slope_ref[g * q_per_kv + hh], slope)
        sink = jnp.where(hrow == hh, sink_ref[g * q_per_kv + hh], sink)
    scale = 1.0 / math.sqrt(HEAD_DIM)
    q = jnp.concatenate([q_ref[:, hh * HEAD_DIM:(hh + 1) * HEAD_DIM] for hh in range(q_per_kv)], axis=0)
    q = (q.astype(F32) * scale).astype(BF16)
    km = km_ref[...].astype(BF16)
    vm = vm_ref[...].astype(BF16)
    kn = kn_ref[...].astype(BF16)
    vn = vn_ref[...].astype(BF16)

    def store(o):
        o_ref[...] = jnp.concatenate([o[hh * t:(hh + 1) * t] for hh in range(q_per_kv)], axis=1)

    @pl.when(i < nb)
    def _():
        d_prev = (t + a - c).astype(F32)
        d_cur = jnp.abs(a - c).astype(F32)
        d_next = (t + c - a).astype(F32)
        ok_prev = c >= a + jnp.where(i > 0, 0, t)
        ok_next = c <= a - jnp.where(i < nb - 1, 0, t)
        kp, kc = kp_ref[...].astype(BF16), kc_ref[...].astype(BF16)
        vp, vc = vp_ref[...].astype(BF16), vc_ref[...].astype(BF16)
        s_m = _qk(q, km)
        s_p = jnp.where(ok_prev, _qk(q, kp) - slope * d_prev, NEG)
        s_c = _qk(q, kc) - slope * d_cur
        s_n = jnp.where(ok_next, _qk(q, kn) - slope * d_next, NEG)
        store(_softmax_pv([(s_m, vm), (s_p, vp), (s_c, vc), (s_n, vn)], sink))

    @pl.when(i == nb)
    def _():
        delta = n_meta + c - a
        s_m = _qk(q, km)
        s_0 = jnp.where(delta <= t, _qk(q, kn) - slope * delta.astype(F32), NEG)
        store(_softmax_pv([(s_m, vm), (s_0, vn)], sink))


def _windowed_attention(proj3, sink, q_col, k_col, v_col, n_q_heads, n_real, n_meta):
    b, l, _ = proj3.shape
    nb = n_real // ATT_BLOCK
    q_per_kv = n_q_heads // N_KV_HEADS
    qw = q_per_kv * HEAD_DIM
    qb, kb, vb = q_col // qw, k_col // HEAD_DIM, v_col // HEAD_DIM
    mb = n_real // n_meta
    slopes = jnp.asarray([2.0 ** (-8.0 * (h + 1) / n_q_heads) for h in range(n_q_heads)], F32)
    smem = pl.BlockSpec(memory_space=pltpu.SMEM)

    def kv_blk(colb, shift):
        def index(bi, g, i):
            blk = jnp.clip(i + shift, 0, nb - 1)
            if shift == 1:
                blk = jnp.where(i == nb, 0, blk)
            return bi, blk, colb + g
        return pl.BlockSpec((None, ATT_BLOCK, HEAD_DIM), index)

    kv_meta = lambda colb: pl.BlockSpec((None, n_meta, HEAD_DIM), lambda bi, g, i: (bi, mb, colb + g))
    return pl.pallas_call(
        functools.partial(_attn_kernel, nb=nb, n_meta=n_meta, q_per_kv=q_per_kv),
        grid=(b, N_KV_HEADS, nb + 1),
        in_specs=[smem, smem,
                  pl.BlockSpec((None, ATT_BLOCK, qw), lambda bi, g, i: (bi, i, qb + g)),
                  kv_blk(kb, -1), kv_blk(kb, 0), kv_blk(kb, 1), kv_meta(kb),
                  kv_blk(vb, -1), kv_blk(vb, 0), kv_blk(vb, 1), kv_meta(vb)],
        out_specs=pl.BlockSpec((None, ATT_BLOCK, qw), lambda bi, g, i: (bi, i, g)),
        out_shape=jax.ShapeDtypeStruct((b, l, n_q_heads * HEAD_DIM), F32),
        compiler_params=_params(("arbitrary", "arbitrary", "arbitrary")),
    )(slopes, sink, proj3, proj3, proj3, proj3, proj3, proj3, proj3, proj3, proj3)


def _group_norm_kernel(ya_ref, yb_ref, yc_ref, yd_ref, g_ref, lng_ref, lnb_ref, o_ref, *, widths):
    def rms(y, g):
        return y * lax.rsqrt(jnp.mean(y * y, axis=-1, keepdims=True) + EPS) * g

    off = 0
    for idx, (y_ref, w) in enumerate(zip((ya_ref, yb_ref, yc_ref, yd_ref), widths)):
        y = y_ref[...]
        if idx == 2:
            mu = jnp.mean(y, axis=-1, keepdims=True)
            yc = y - mu
            var = jnp.mean(yc * yc, axis=-1, keepdims=True)
            y = yc * lax.rsqrt(var + EPS) * lng_ref[...] + lnb_ref[...]
            y = y * jax.nn.sigmoid(y)
        o_ref[:, off:off + w] = rms(y, g_ref[:, off:off + w]).astype(o_ref.dtype)
        off += w


def _group_norms(ya, yb, yc, yd, group_g, ln_g, ln_b, tm=256):
    m = ya.shape[0]
    widths = (ya.shape[1], yb.shape[1], yc.shape[1], yd.shape[1])
    tot = sum(widths)
    row = lambda w: pl.BlockSpec((tm, w), lambda i: (i, 0))
    full = lambda w: pl.BlockSpec((1, w), lambda i: (0, 0))
    return pl.pallas_call(
        functools.partial(_group_norm_kernel, widths=widths),
        grid=(pl.cdiv(m, tm),),
        in_specs=[row(widths[0]), row(widths[1]), row(widths[2]), row(widths[3]),
                  full(tot), full(widths[2]), full(widths[2])],
        out_specs=row(tot),
        out_shape=jax.ShapeDtypeStruct((m, tot), BF16),
        compiler_params=_params(("arbitrary",)),
    )(ya, yb, yc, yd, group_g.reshape(1, tot), ln_g.reshape(1, -1), ln_b.reshape(1, -1))


def _norm_router_kernel(h_ref, g_ref, w_ref, o_ref):
    x = h_ref[...]
    ms = jnp.mean(x * x, axis=-1, keepdims=True)
    xn = (x * lax.rsqrt(ms + EPS) * g_ref[...]).astype(BF16)
    o_ref[...] = lax.dot_general(w_ref[...], xn, (((1,), (1,)), ((), ())), preferred_element_type=F32)


def _norm_router(h3, g, router_w, n_real, tm=256):
    b, _, d = h3.shape
    nt = n_real // tm
    wt = router_w.T.astype(BF16)
    return pl.pallas_call(
        _norm_router_kernel,
        grid=(b, nt),
        in_specs=[pl.BlockSpec((None, tm, d), lambda bi, i: (bi, i, 0)),
                  pl.BlockSpec((1, d), lambda bi, i: (0, 0)),
                  pl.BlockSpec((N_EXPERTS, d), lambda bi, i: (0, 0))],
        out_specs=pl.BlockSpec((N_EXPERTS, tm), lambda bi, i: (0, bi * nt + i)),
        out_shape=jax.ShapeDtypeStruct((N_EXPERTS, b * n_real), F32),
        compiler_params=_params(("arbitrary", "arbitrary")),
    )(h3, g.reshape(1, d), wt)


def _route_kernel(lg_ref, b_ref, slot1_ref, slot2_ref, g1_ref, g2_ref, te_ref, tv_ref, nx_ref, fp_ref, *,
                  chunk, n_slots):
    ne, m = lg_ref.shape
    logits = lg_ref[...] + b_ref[...]
    row = lax.broadcasted_iota(jnp.int32, (ne, m), 0)
    m1 = jnp.max(logits, axis=0, keepdims=True)
    e1 = jnp.min(jnp.where(logits == m1, row, ne), axis=0, keepdims=True)
    rest = jnp.where(row == e1, -jnp.inf, logits)
    m2 = jnp.max(rest, axis=0, keepdims=True)
    e2 = jnp.min(jnp.where(rest == m2, row, ne), axis=0, keepdims=True)
    g1 = 1.0 / (1.0 + jnp.exp(m2 - m1))
    g1_ref[...] = g1
    g2_ref[...] = 1.0 - g1
    oh1 = (row == e1).astype(F32)
    oh2 = (row == e2).astype(F32)
    c1 = jnp.sum(oh1, axis=1, keepdims=True)
    c2 = jnp.sum(oh2, axis=1, keepdims=True)
    padded = jnp.floor((c1 + c2 + (MOE_TILE - 1)) / MOE_TILE) * MOE_TILE
    erow = lax.broadcasted_iota(jnp.int32, (ne, 1), 0)
    start = jnp.zeros((ne, 1), F32)
    for e in range(ne - 1):
        start = start + jnp.where(erow > e, padded[e:e + 1, :], 0.0)
    end = start + padded
    tri = (lax.broadcasted_iota(jnp.int32, (chunk, chunk), 0)
           < lax.broadcasted_iota(jnp.int32, (chunk, chunk), 1)).astype(BF16)
    carry = jnp.concatenate([jnp.zeros((ne, 1), F32), c1], axis=0)
    base1 = start
    for ci in range(m // chunk):
        sl = slice(ci * chunk, (ci + 1) * chunk)
        oh = jnp.concatenate([oh1[:, sl], oh2[:, sl]], axis=0)
        pre = jnp.dot(oh.astype(BF16), tri, preferred_element_type=F32) + carry
        carry = carry + jnp.sum(oh, axis=1, keepdims=True)
        s1 = jnp.sum(oh[:ne] * (pre[:ne] + base1), axis=0, keepdims=True)
        s2 = jnp.sum(oh[ne:] * (pre[ne:] + base1), axis=0, keepdims=True)
        slot1_ref[:, sl] = s1.astype(jnp.int32)
        slot2_ref[:, sl] = s2.astype(jnp.int32)
    tile0 = (lax.broadcasted_iota(jnp.int32, (1, V7X_LANES), 1) * MOE_TILE).astype(F32)
    te = jnp.sum((end <= tile0).astype(jnp.int32), axis=0, keepdims=True)
    nonempty = padded > 0.0
    last = jnp.max(jnp.where(nonempty, erow, 0), axis=0, keepdims=True)
    te = jnp.minimum(te, last)
    te_ref[...] = te
    tv_ref[...] = (tile0 < end[ne - 1:ne, :]).astype(jnp.int32)
    nx_ref[...] = jnp.min(jnp.where(nonempty & (erow > te), erow, ne), axis=0, keepdims=True)
    lane = lax.broadcasted_iota(jnp.int32, (ne, V7X_LANES), 1)
    first_pad = jnp.floor((start + c1 + c2) / PAD_FILL_ALIGN) * PAD_FILL_ALIGN
    first_pad = jnp.minimum(first_pad, float(n_slots - PAD_FILL_ROWS)).astype(jnp.int32)
    fp_ref[...] = jnp.sum(jnp.where(lane == erow, first_pad, 0), axis=0, keepdims=True)


def _route(logits_t, router_b, n_slots):
    ne, m = logits_t.shape
    assert n_slots // MOE_TILE <= V7X_LANES
    vec = lambda dt: jax.ShapeDtypeStruct((1, m), dt)
    lane = jax.ShapeDtypeStruct((1, V7X_LANES), jnp.int32)
    return pl.pallas_call(
        functools.partial(_route_kernel, chunk=_pick_tile(m, (512, 256, 128)), n_slots=n_slots),
        out_shape=[vec(jnp.int32), vec(jnp.int32), vec(F32), vec(F32), lane, lane, lane, lane],
        compiler_params=_params(None),
    )(logits_t, router_b.reshape(ne, 1))


def _row_copy(src_hbm, dst_vmem, sem, src_row, dst_row):
    return pltpu.make_async_copy(src_hbm.at[pl.ds(src_row, 1)], dst_vmem.at[pl.ds(dst_row, 1)], sem)


def _norm_scatter_kernel(s1_ref, s2_ref, fp_ref, tv_ref, h_ref, g_ref, xs_hbm, pk_ref, zero_ref, sem, zsem, *,
                         n_steps, n_tiles):
    t = pl.program_id(0) * pl.num_programs(1) + pl.program_id(1)

    @pl.when(t == 0)
    def _():
        zero_ref[...] = jnp.zeros_like(zero_ref)

        def fill(row, n):
            return pltpu.make_async_copy(zero_ref.at[pl.ds(0, n)], xs_hbm.at[pl.ds(row, n)], zsem)

        def fill_now(row, n):
            fill(row, n).start()
            fill(row, n).wait()

        for e in range(N_EXPERTS):
            fill_now(pl.multiple_of(fp_ref[e], PAD_FILL_ALIGN), PAD_FILL_ROWS)
        for tile in range(n_tiles):
            @pl.when(tv_ref[tile] == 0)
            def _():
                fill_now(tile * MOE_TILE, MOE_TILE)

    cur = t % 2
    tm = h_ref.shape[0]
    x = h_ref[...]
    ms = jnp.mean(x * x, axis=-1, keepdims=True)
    pk_ref[cur] = x * lax.rsqrt(ms + EPS) * g_ref[...]

    def row_out(buf, r, slot):
        return pltpu.make_async_copy(pk_ref.at[buf, pl.ds(r, 1)], xs_hbm.at[pl.ds(slot, 1)], sem.at[buf])

    def start(r4, carry):
        for k in range(ROW_DMA_UNROLL):
            r = r4 * ROW_DMA_UNROLL + k
            row_out(cur, r, s1_ref[0, 0, r]).start()
            row_out(cur, r, s2_ref[0, 0, r]).start()
        return carry

    lax.fori_loop(0, tm // ROW_DMA_UNROLL, start, 0)

    def wait_all(buf):
        def wait(r, carry):
            row_out(buf, r, 0).wait()
            row_out(buf, r, 0).wait()
            return carry

        lax.fori_loop(0, tm, wait, 0)

    @pl.when(t > 0)
    def _():
        wait_all(1 - cur)

    @pl.when(t == n_steps - 1)
    def _():
        wait_all(cur)


def _norm_scatter(h3, g, slot1, slot2, first_pad, tile_valid, n_real, n_slots, tm=256):
    b, _, d = h3.shape
    nt = n_real // tm
    idx = lambda a: a.reshape(b * nt, 1, tm)
    smem_blk = pl.BlockSpec((1, 1, tm), lambda bi, i: (bi * nt + i, 0, 0), memory_space=pltpu.SMEM)
    smem = pl.BlockSpec(memory_space=pltpu.SMEM)
    return pl.pallas_call(
        functools.partial(_norm_scatter_kernel, n_steps=b * nt, n_tiles=n_slots // MOE_TILE),
        grid=(b, nt),
        in_specs=[smem_blk, smem_blk, smem, smem,
                  pl.BlockSpec((None, tm, d), lambda bi, i: (bi, i, 0)),
                  pl.BlockSpec((1, d), lambda bi, i: (0, 0))],
        out_specs=pl.BlockSpec(memory_space=pl.ANY),
        out_shape=jax.ShapeDtypeStruct((n_slots, d), F32),
        scratch_shapes=[pltpu.VMEM((2, tm, d), F32), pltpu.VMEM((PAD_FILL_ROWS, d), F32),
                        pltpu.SemaphoreType.DMA((2,)), pltpu.SemaphoreType.DMA(())],
        compiler_params=_params(("arbitrary", "arbitrary")),
    )(idx(slot1), idx(slot2), first_pad, tile_valid, h3, g.reshape(1, d))


def _combine_kernel(s1_ref, s2_ref, n1_ref, n2_ref, h_ref, g1_ref, g2_ref, fg_ref, y_hbm, o_ref,
                    a_ref, b_ref, sem, *, n_steps):
    t = pl.program_id(0) * pl.num_programs(1) + pl.program_id(1)
    cur = t % 2
    n = a_ref.shape[1]

    def gather(buf, i1_ref, i2_ref):
        def start(r4, carry):
            for k in range(ROW_DMA_UNROLL):
                r = r4 * ROW_DMA_UNROLL + k
                _row_copy(y_hbm, a_ref.at[buf], sem.at[buf], i1_ref[0, 0, r], r).start()
                _row_copy(y_hbm, b_ref.at[buf], sem.at[buf], i2_ref[0, 0, r], r).start()
            return carry

        lax.fori_loop(0, n // ROW_DMA_UNROLL, start, 0)

    @pl.when(t == 0)
    def _():
        gather(0, s1_ref, s2_ref)

    @pl.when(t + 1 < n_steps)
    def _():
        gather(1 - cur, n1_ref, n2_ref)

    def wait(r, carry):
        _row_copy(y_hbm, a_ref.at[cur], sem.at[cur], 0, r).wait()
        _row_copy(y_hbm, b_ref.at[cur], sem.at[cur], 0, r).wait()
        return carry

    lax.fori_loop(0, n, wait, 0)
    h = h_ref[...] + g1_ref[...] * a_ref[cur] + g2_ref[...] * b_ref[cur]
    ms = jnp.mean(h * h, axis=-1, keepdims=True)
    o_ref[...] = h * lax.rsqrt(ms + EPS) * fg_ref[...]


def _moe_combine_final(h3, y, slot1, slot2, g1, g2, final_g, n_real, tm=256):
    b, _, d = h3.shape
    nt = n_real // tm
    idx = lambda a: a.reshape(b * nt, 1, tm)
    gate = lambda a: a.reshape(b, n_real, 1)
    smem_blk = pl.BlockSpec((1, 1, tm), lambda bi, i: (bi * nt + i, 0, 0), memory_space=pltpu.SMEM)
    smem_next = pl.BlockSpec((1, 1, tm), lambda bi, i: (jnp.minimum(bi * nt + i + 1, b * nt - 1), 0, 0),
                             memory_space=pltpu.SMEM)
    return pl.pallas_call(
        functools.partial(_combine_kernel, n_steps=b * nt),
        grid=(b, nt),
        in_specs=[smem_blk, smem_blk, smem_next, smem_next,
                  pl.BlockSpec((None, tm, d), lambda bi, i: (bi, i, 0)),
                  pl.BlockSpec((None, tm, 1), lambda bi, i: (bi, i, 0)),
                  pl.BlockSpec((None, tm, 1), lambda bi, i: (bi, i, 0)),
                  pl.BlockSpec((1, d), lambda bi, i: (0, 0)),
                  pl.BlockSpec(memory_space=pl.ANY)],
        out_specs=pl.BlockSpec((None, tm, d), lambda bi, i: (bi, i, 0)),
        out_shape=jax.ShapeDtypeStruct((b, n_real, d), F32),
        scratch_shapes=[pltpu.VMEM((2, tm, d), F32), pltpu.VMEM((2, tm, d), F32),
                        pltpu.SemaphoreType.DMA((2,))],
        compiler_params=_params(("arbitrary", "arbitrary")),
    )(idx(slot1), idx(slot2), idx(slot1), idx(slot2), h3, gate(g1), gate(g2), final_g.reshape(1, d), y)


def _filter_features(l, lhp, n_meta, n_emb):
    j = np.arange(lhp)
    r = np.concatenate([2 * j, 2 * j + 1])
    lp = 2 * lhp
    pos = (r + n_meta) % l
    valid = r < l
    t = pos / (l - 1.0)
    bands = (n_emb - 1) // 2
    w = 2.0 * np.pi * pos / l
    f = np.linspace(1e-4, bands - 1, bands)
    z = np.zeros((lp, V7X_LANES), np.float64)
    z[:, 0] = t
    z[:, 1:1 + bands] = np.cos(f[None, :] * w[:, None])
    z[:, 1 + bands:1 + 2 * bands] = -np.sin(f[None, :] * w[:, None])
    z *= valid[:, None]
    z[:, V7X_LANES - 1] = valid
    return jnp.asarray(z, F32)


def kernel(x, meta_tokens, norm_mix_g, w_in, hy_conv_w, hy_conv_b, hy_f_w1, hy_f_b1, hy_f_w2, hy_f_b2, hy_f_w3, hy_f_b3, hy_f_wo, hy_f_freq, hy_decay, hy_skip, cv_dw_w, cv_dw_b, cv_ln_g, cv_ln_b, attn_sink, group_norm_g, w_out, norm_ffn_g, ffn_w1, ffn_w3, ffn_w2, router_w, router_b, moe_w1, moe_w3, moe_w2, final_norm_g):
    b, n_real, d = x.shape
    n_meta = meta_tokens.shape[0]
    depth = w_in.shape[0]
    l = n_real + n_meta
    m = b * l
    w_hy = hy_skip.shape[1]
    w_fn = w_hy
    w_cv = cv_dw_b.shape[1]
    kv_w = N_KV_HEADS * HEAD_DIM
    p_in = w_in.shape[2]
    w_at = p_in - 3 * w_hy - w_fn - 2 * w_cv - 2 * kv_w
    n_q_heads = w_at // HEAD_DIM
    d_ff = ffn_w1.shape[2]
    assert depth == 2 and router_w.shape[2] == N_EXPERTS
    assert n_meta == V7X_BF16_ROWS and n_real % 256 == 0
    off_fn = 3 * w_hy
    off_cv = off_fn + w_fn
    off_q = off_cv + 2 * w_cv
    off_k = off_q + w_at
    off_v = off_k + kv_w

    lh = l // 2
    lhp = _round_up(lh + 1, V7X_LANES)
    j = np.arange(lhp)
    freq = np.arange(lhp)
    hy_unit, hy_period = np.pi / l, 2 * l
    tabs_f, tabs_t, tabs_n = [], [], []
    for par in (0, 1):
        col_pos = (2 * j + par + n_meta) % l
        tabs_f.append(_dft_tables(col_pos, j < lh, hy_unit, hy_period, 0, 1, lhp // 16 - 1,
                                  np.arange(lhp - 16, lhp), lhp))
        tabs_t.append(_dft_tables(freq, freq <= lh, hy_unit, hy_period, n_meta + par, 2, n_real // 32,
                                  np.arange(par, n_meta, 2), lhp))
        tabs_n.append(_dft_tables(col_pos, j < lh, 2 * np.pi / l, l, n_meta, 1, lh // 16,
                                  n_meta + 16 * (lh // 16) + np.arange(lh % 16), lhp))
    edge = np.where(freq == 0, 1.0, 2.0) / (2.0 * l)
    w1 = jnp.asarray(np.where(freq <= lh, edge, 0.0).reshape(lhp, 1), F32)
    w2 = jnp.asarray(np.where(freq < lh, edge, 0.0).reshape(lhp, 1), F32)
    zc = _filter_features(l, lhp, n_meta, hy_f_w1.shape[1])

    h = jnp.concatenate([x, jnp.broadcast_to(meta_tokens[None].astype(x.dtype), (b, n_meta, d))], axis=1)
    h = h.reshape(m, d)
    out = None
    for layer in range(depth):
        xn = _rmsnorm(h, norm_mix_g[layer], BF16)
        proj = _matmul(xn, [w_in], layer, tk=d, k_blk=0, n_out=p_in, tn=512, tm=MM_ROWS, out_dtype=BF16)
        proj3 = proj.reshape(b, l, p_in)

        hs, hd = _hyena_filter_sums(zc, hy_f_w1[layer], hy_f_b1[layer], hy_f_w2[layer], hy_f_b2[layer],
                                    hy_f_w3[layer], hy_f_b3[layer], hy_f_wo[layer], hy_f_freq[layer],
                                    hy_decay[layer], w_hy)
        kspec = _hyena_filter_spectrum(tabs_f[0], tabs_f[1], hs, hd, w1, w2)
        x0c, vxe, vxo = _hyena_prep(proj3, hy_conv_w[layer], hy_conv_b[layer], w_hy, n_real, n_meta, lhp)
        spec = _hyena_fwd(tabs_f[0], tabs_f[1], vxe, vxo, kspec)
        y_a = _hyena_inv(tabs_t[0], tabs_t[1], spec, vxe, vxo, x0c, hy_skip[layer])

        y_b = _fnet_time(tabs_n[0], tabs_n[1], _fnet_chan(proj3, off_fn // w_fn, w_fn, lhp), l)

        y_c = _conformer_conv(proj3, off_cv // 128, cv_dw_w[layer], cv_dw_b[layer], w_cv, n_real, n_meta)
        y_d = _windowed_attention(proj3, attn_sink[layer], off_q, off_k, off_v, n_q_heads, n_real, n_meta)

        ycat = _group_norms(y_a.reshape(m, w_hy), y_b.reshape(m, w_fn), y_c.reshape(m, w_cv),
                            y_d.reshape(m, w_at), group_norm_g[layer], cv_ln_g[layer], cv_ln_b[layer])
        h = _matmul(ycat, [w_out], layer, tk=ycat.shape[1], k_blk=0, n_out=d, tn=512, tm=MM_ROWS,
                    out_dtype=F32, residual=h)

        if layer % 2 == 0:
            i = layer // 2
            xn = _rmsnorm(h, norm_ffn_g[layer], BF16)
            tn_ff = _pick_tile(d_ff, (512, 256, 128))
            act = _matmul(xn, [ffn_w1, ffn_w3], i, tk=d, k_blk=0, n_out=d_ff, tn=tn_ff, tm=MM_ROWS,
                          out_dtype=BF16, swiglu=True)
            n_k = 2 if d_ff % (2 * V7X_LANES) == 0 and d_ff > 4096 else 1
            for kb in range(n_k):
                h = _matmul(act, [ffn_w2], i, tk=d_ff // n_k, k_blk=kb, n_out=d, tn=512, tm=1040,
                            out_dtype=F32, residual=h)
        else:
            i = layer // 2
            h3 = h.reshape(b, l, d)
            mr = b * n_real
            logits_t = _norm_router(h3, norm_ffn_g[layer], router_w[i], n_real)
            n_slots = _round_up(2 * mr + N_EXPERTS * (MOE_TILE - 1), MOE_TILE)
            n_tiles = n_slots // MOE_TILE
            slot1, slot2, g1, g2, te, tv, nx, fp = _route(logits_t, router_b[i], n_slots)
            slot1, slot2 = slot1.reshape(mr), slot2.reshape(mr)
            te, tv, nx, fp = (a.reshape(-1)[:n_tiles] for a in (te, tv, nx, fp))
            xs = _norm_scatter(h3, norm_ffn_g[layer], slot1, slot2, fp, tv, n_real, n_slots)
            act = _grouped_matmul(xs, [moe_w1[i], moe_w3[i]], te, tv, nx, tn=512, out_dtype=BF16, swiglu=True)
            ys = _grouped_matmul(act, [moe_w2[i]], te, tv, nx, tn=1024, out_dtype=F32)
            out = _moe_combine_final(h3, ys, slot1, slot2, g1, g2, final_norm_g, n_real)
    return out
```

```python
import functools
import math

import numpy as np
import jax
import jax.numpy as jnp
from jax import lax
from jax.experimental import pallas as pl
from jax.experimental.pallas import tpu as pltpu

F32 = jnp.float32
BF16 = jnp.bfloat16

V7X_LANES = 128
V7X_BF16_ROWS = 16
V7X_VMEM_LIMIT = 58 * 1024 * 1024

EPS = 1e-6
NEG = -1e30
HEAD_DIM = 128
N_KV_HEADS = 2
ATT_BLOCK = 128
FNET_GROUP_W = 256
N_EXPERTS = 8
MOE_TILE = 512
ROW_DMA_UNROLL = 8
PAD_FILL_ALIGN = 8
PAD_FILL_ROWS = MOE_TILE + PAD_FILL_ALIGN
MM_ROWS = 1376


def _round_up(x, m):
    return (x + m - 1) // m * m


def _pick_tile(n, candidates):
    for c in candidates:
        if n % c == 0:
            return c
    raise ValueError(f"no tile for {n} in {candidates}")


def _params(sem, vmem=None):
    if sem is None:
        return pltpu.CompilerParams(vmem_limit_bytes=vmem or V7X_VMEM_LIMIT)
    return pltpu.CompilerParams(dimension_semantics=sem, vmem_limit_bytes=vmem or V7X_VMEM_LIMIT)


def _rmsnorm_kernel(x_ref, g_ref, o_ref):
    x = x_ref[...]
    ms = jnp.mean(x * x, axis=-1, keepdims=True)
    o_ref[...] = (x * lax.rsqrt(ms + EPS) * g_ref[...]).astype(o_ref.dtype)


def _rmsnorm(x2d, g, out_dtype, tm=256):
    m, d = x2d.shape
    return pl.pallas_call(
        _rmsnorm_kernel,
        grid=(pl.cdiv(m, tm),),
        in_specs=[pl.BlockSpec((tm, d), lambda i: (i, 0)), pl.BlockSpec((1, d), lambda i: (0, 0))],
        out_specs=pl.BlockSpec((tm, d), lambda i: (i, 0)),
        out_shape=jax.ShapeDtypeStruct((m, d), out_dtype),
        compiler_params=_params(("arbitrary",)),
    )(x2d, g.reshape(1, d))


def _cast_tile(w_ref, wb_ref):
    k = w_ref.shape[0]
    ck = _pick_tile(k, (256, 128, 64, 32, 16))

    def body(c, carry):
        r = pl.multiple_of(c * ck, ck)
        wb_ref[pl.ds(r, ck), :] = w_ref[pl.ds(r, ck), :].astype(BF16)
        return carry

    lax.fori_loop(0, k // ck, body, 0)


def _mm_kernel(x_ref, *refs, n_w, swiglu, has_res, layer, k_off, tn):
    w_hbm = refs[:n_w]
    res_ref = refs[n_w] if has_res else None
    o_ref, stage_ref, wb_ref, sem = refs[n_w + int(has_res):n_w + int(has_res) + 4]
    j = pl.program_id(0)
    i = pl.program_id(1)
    tk = stage_ref.shape[1]

    def w_copy(idx, col_tile):
        col = pl.multiple_of(col_tile * tn, tn)
        return pltpu.make_async_copy(w_hbm[idx].at[layer, pl.ds(k_off, tk), pl.ds(col, tn)],
                                     stage_ref.at[idx], sem.at[idx])

    @pl.when((j == 0) & (i == 0))
    def _():
        for idx in range(n_w):
            w_copy(idx, 0).start()

    @pl.when(i == 0)
    def _():
        for idx in range(n_w):
            w_copy(idx, j).wait()
            _cast_tile(stage_ref.at[idx], wb_ref.at[idx])

        @pl.when(j + 1 < pl.num_programs(0))
        def _():
            for idx in range(n_w):
                w_copy(idx, j + 1).start()

    x = x_ref[...]
    acc = jnp.dot(x, wb_ref[0], preferred_element_type=F32)
    if swiglu:
        acc = acc * jax.nn.sigmoid(acc) * jnp.dot(x, wb_ref[1], preferred_element_type=F32)
    if has_res:
        acc = acc + res_ref[...]
    o_ref[...] = acc.astype(o_ref.dtype)


def _matmul(x, ws, layer, *, tk, k_blk, n_out, tn, tm, out_dtype, residual=None, swiglu=False):
    m = x.shape[0]
    n_w = len(ws)
    tm = min(tm, _round_up(m, V7X_BF16_ROWS))
    in_specs = [pl.BlockSpec((tm, tk), lambda j, i: (i, k_blk))]
    in_specs += [pl.BlockSpec(memory_space=pl.ANY) for _ in ws]
    args = [x, *ws]
    if residual is not None:
        in_specs.append(pl.BlockSpec((tm, tn), lambda j, i: (i, j)))
        args.append(residual)
    kern = functools.partial(_mm_kernel, n_w=n_w, swiglu=swiglu, has_res=residual is not None,
                             layer=layer, k_off=k_blk * tk, tn=tn)
    return pl.pallas_call(
        kern,
        grid=(n_out // tn, pl.cdiv(m, tm)),
        in_specs=in_specs,
        out_specs=pl.BlockSpec((tm, tn), lambda j, i: (i, j)),
        out_shape=jax.ShapeDtypeStruct((m, n_out), out_dtype),
        scratch_shapes=[pltpu.VMEM((n_w, tk, tn), F32), pltpu.VMEM((n_w, tk, tn), BF16),
                        pltpu.SemaphoreType.DMA((n_w,))],
        compiler_params=_params(("arbitrary", "arbitrary")),
    )(*args)


def _gmm_kernel(te_ref, tv_ref, nx_ref, x_ref, *refs, n_w, swiglu, tn):
    w_hbm = refs[:n_w]
    o_ref, stage_ref, wb_ref, sem = refs[n_w:n_w + 4]
    j = pl.program_id(0)
    t = pl.program_id(1)
    n_j = pl.num_programs(0)
    e = te_ref[t]

    def w_copy(i, expert, col_tile):
        col = pl.multiple_of(col_tile * tn, tn)
        return pltpu.make_async_copy(w_hbm[i].at[expert, :, pl.ds(col, tn)], stage_ref.at[i], sem.at[i])

    @pl.when((j == 0) & (t == 0))
    def _():
        for i in range(n_w):
            w_copy(i, e, 0).start()

    @pl.when((t == 0) | (e != te_ref[jnp.maximum(t - 1, 0)]))
    def _():
        for i in range(n_w):
            w_copy(i, e, j).wait()
            _cast_tile(stage_ref.at[i], wb_ref.at[i])
        e_next = nx_ref[t]

        @pl.when(e_next < N_EXPERTS)
        def _():
            for i in range(n_w):
                w_copy(i, e_next, j).start()

        @pl.when((e_next >= N_EXPERTS) & (j + 1 < n_j))
        def _():
            for i in range(n_w):
                w_copy(i, te_ref[0], j + 1).start()

    @pl.when(tv_ref[t] != 0)
    def _():
        x = x_ref[...].astype(BF16)
        acc = jnp.dot(x, wb_ref[0], preferred_element_type=F32)
        if swiglu:
            acc = acc * jax.nn.sigmoid(acc) * jnp.dot(x, wb_ref[1], preferred_element_type=F32)
        o_ref[...] = acc.astype(o_ref.dtype)

    @pl.when(tv_ref[t] == 0)
    def _():
        o_ref[...] = jnp.zeros_like(o_ref)


def _grouped_matmul(x, ws, tile_expert, tile_valid, next_expert, *, tn, out_dtype, swiglu=False):
    s, k = x.shape
    n = ws[0].shape[2]
    n_w = len(ws)
    grid_spec = pltpu.PrefetchScalarGridSpec(
        num_scalar_prefetch=3,
        grid=(n // tn, s // MOE_TILE),
        in_specs=[pl.BlockSpec((MOE_TILE, k), lambda j, t, te, tv, nx: (t, 0))]
        + [pl.BlockSpec(memory_space=pl.ANY) for _ in ws],
        out_specs=pl.BlockSpec((MOE_TILE, tn), lambda j, t, te, tv, nx: (t, j)),
        scratch_shapes=[pltpu.VMEM((n_w, k, tn), F32), pltpu.VMEM((n_w, k, tn), BF16),
                        pltpu.SemaphoreType.DMA((n_w,))],
    )
    return pl.pallas_call(
        functools.partial(_gmm_kernel, n_w=n_w, swiglu=swiglu, tn=tn),
        grid_spec=grid_spec,
        out_shape=jax.ShapeDtypeStruct((s, n), out_dtype),
        compiler_params=_params(("arbitrary", "arbitrary")),
    )(tile_expert, tile_valid, next_expert, x, *ws)


def _dft_table_kernel(ac_ref, as_ref, bc_ref, bs_ref, cc_ref, cs_ref, tc_ref, ts_ref, c_ref, s_ref, *,
                      n_blocks, n_rows_out):
    ac = ac_ref[...]
    asn = as_ref[...]

    def body(q, carry):
        v = q % 16
        u = q // 16
        bc, bs = bc_ref[pl.ds(v, 1), :], bs_ref[pl.ds(v, 1), :]
        cc, cs = cc_ref[pl.ds(u, 1), :], cs_ref[pl.ds(u, 1), :]
        dc = bc * cc - bs * cs
        dsn = bs * cc + bc * cs
        row = pl.multiple_of(q * 16, 16)
        c_ref[pl.ds(row, 16), :] = (ac * dc - asn * dsn).astype(BF16)
        s_ref[pl.ds(row, 16), :] = (asn * dc + ac * dsn).astype(BF16)
        return carry

    lax.fori_loop(0, n_blocks, body, 0)
    c_ref[pl.ds(16 * n_blocks, 16), :] = tc_ref[...].astype(BF16)
    s_ref[pl.ds(16 * n_blocks, 16), :] = ts_ref[...].astype(BF16)
    pad = n_rows_out - 16 * (n_blocks + 1)
    if pad:
        c_ref[pl.ds(16 * (n_blocks + 1), pad), :] = jnp.zeros((pad, c_ref.shape[1]), BF16)
        s_ref[pl.ds(16 * (n_blocks + 1), pad), :] = jnp.zeros((pad, s_ref.shape[1]), BF16)


def _dft_tables(col_pos, col_valid, unit, period, row0, row_step, n_blocks, tail_pos, n_rows_out):
    ncol = col_pos.shape[0]
    n_u = -(-n_blocks // 16)
    p = col_pos.astype(np.int64)[None, :]
    valid = col_valid[None, :].astype(np.float64)
    assert len(tail_pos) <= 16 and n_rows_out >= 16 * (n_blocks + 1)

    def factor(rowpos, mask):
        ang = unit * ((np.asarray(rowpos, dtype=np.int64)[:, None] * p) % period)
        return (jnp.asarray((np.cos(ang) * mask).astype(np.float32)),
                jnp.asarray((np.sin(ang) * mask).astype(np.float32)))

    a_c, a_s = factor(row0 + row_step * np.arange(16), valid)
    b_c, b_s = factor(row_step * 16 * np.arange(16), 1.0)
    c_c, c_s = factor(row_step * 256 * np.arange(n_u), 1.0)
    tail = np.zeros(16, np.int64)
    tail[:len(tail_pos)] = tail_pos
    t_c, t_s = factor(tail, valid * (np.arange(16) < len(tail_pos))[:, None])
    pc = _pick_tile(ncol, (2176, 1408, 384, 256, 128))
    kern = functools.partial(_dft_table_kernel, n_blocks=n_blocks, n_rows_out=n_rows_out)
    col = lambda rows: pl.BlockSpec((rows, pc), lambda j: (0, j))
    return pl.pallas_call(
        kern,
        grid=(ncol // pc,),
        in_specs=[col(16), col(16), col(16), col(16), col(n_u), col(n_u), col(16), col(16)],
        out_specs=[col(n_rows_out), col(n_rows_out)],
        out_shape=[jax.ShapeDtypeStruct((n_rows_out, ncol), BF16)] * 2,
        compiler_params=_params(("arbitrary",)),
    )(a_c, a_s, b_c, b_s, c_c, c_s, t_c, t_s)


def _filter_kernel(z_ref, w1_ref, b1_ref, w2_ref, b2_ref, w3_ref, b3_ref, wo_ref, fr_ref, dec_ref,
                   hs_ref, hd_ref, *, width):
    hp = lax.Precision.HIGHEST
    z = z_ref[...]
    fr = fr_ref[...]
    hid = jnp.sin(fr * (jnp.dot(z, w1_ref[...], precision=hp, preferred_element_type=F32) + b1_ref[...]))
    hid = jnp.sin(fr * (jnp.dot(hid, w2_ref[...], precision=hp, preferred_element_type=F32) + b2_ref[...]))
    hid = jnp.sin(fr * (jnp.dot(hid, w3_ref[...], precision=hp, preferred_element_type=F32) + b3_ref[...]))
    h = jnp.dot(hid, wo_ref[...], precision=hp, preferred_element_type=F32)
    t = z[:, 0:1]
    valid = z[:, V7X_LANES - 1:V7X_LANES]
    h = h * jnp.exp(-t * dec_ref[...]) * valid
    hf = h[:, :width]
    hb = h[:, width:]
    hs_ref[...] = (hf + hb).astype(BF16)
    hd_ref[...] = (hf - hb).astype(BF16)


def _hyena_filter_sums(zc, w1, b1, w2, b2, w3, b3, wo, freq, decay, width):
    lp = zc.shape[0]
    hid = w2.shape[0]
    tr = _pick_tile(lp, (528, 384, 256, 128))
    w1p = jnp.zeros((V7X_LANES, hid), F32).at[:w1.shape[0]].set(w1)
    full = lambda shape: pl.BlockSpec(shape, lambda i: (0, 0))
    return pl.pallas_call(
        functools.partial(_filter_kernel, width=width),
        grid=(lp // tr,),
        in_specs=[pl.BlockSpec((tr, V7X_LANES), lambda i: (i, 0)),
                  full((V7X_LANES, hid)), full((1, hid)), full((hid, hid)), full((1, hid)),
                  full((hid, hid)), full((1, hid)), full((hid, 2 * width)), full((1, hid)),
                  full((1, 2 * width))],
        out_specs=[pl.BlockSpec((tr, width), lambda i: (i, 0))] * 2,
        out_shape=[jax.ShapeDtypeStruct((lp, width), BF16)] * 2,
        compiler_params=_params(("arbitrary",)),
    )(zc, w1p, b1.reshape(1, hid), w2, b2.reshape(1, hid), w3, b3.reshape(1, hid), wo,
      freq.reshape(1, hid), decay.reshape(1, 2 * width))


def _kf_kernel(ce_ref, se_ref, co_ref, so_ref, hse_ref, hso_ref, hde_ref, hdo_ref, w1_ref, w2_ref,
               k1re_ref, k1im_ref, k2re_ref, k2im_ref):
    ka = jnp.dot(ce_ref[...], hse_ref[...], preferred_element_type=F32)
    kb = jnp.dot(co_ref[...], hso_ref[...], preferred_element_type=F32)
    sa = jnp.dot(se_ref[...], hde_ref[...], preferred_element_type=F32)
    sb = jnp.dot(so_ref[...], hdo_ref[...], preferred_element_type=F32)
    w1 = w1_ref[...]
    w2 = w2_ref[...]
    k1re_ref[...] = w1 * (ka + kb)
    k1im_ref[...] = -w1 * (sa + sb)
    k2re_ref[...] = w2 * (ka - kb)
    k2im_ref[...] = w2 * (sa - sb)


def _half_tile(n):
    return _pick_tile(n, (544, 384, 256, 128))


def _hyena_filter_spectrum(tabs_e, tabs_o, hs, hd, w1, w2):
    (ce, se), (co, so) = tabs_e, tabs_o
    f2p, lhp = ce.shape
    width = hs.shape[1]
    tf = _half_tile(f2p)
    cb = 512
    tab = pl.BlockSpec((tf, lhp), lambda c, f: (f, 0))
    half = lambda which: pl.BlockSpec((lhp, cb), lambda c, f: (which, c))
    wcol = pl.BlockSpec((tf, 1), lambda c, f: (f, 0))
    return pl.pallas_call(
        _kf_kernel,
        grid=(width // cb, f2p // tf),
        in_specs=[tab, tab, tab, tab, half(0), half(1), half(0), half(1), wcol, wcol],
        out_specs=[pl.BlockSpec((tf, cb), lambda c, f: (f, c))] * 4,
        out_shape=[jax.ShapeDtypeStruct((f2p, width), F32)] * 4,
        compiler_params=_params(("arbitrary", "arbitrary")),
    )(ce, se, co, so, hs, hs, hd, hd, w1, w2)


def _hyena_prep_kernel(x0_ref, x1_ref, v_ref, w0_ref, w1_ref, wv_ref, b0_ref, b1_ref, bv_ref,
                       x0c_ref, vxe_ref, vxo_ref, pad_ref, vx_ref, *, n_real, n_meta):
    cb = pad_ref.shape[1]
    l = n_real + n_meta
    pad_ref[pl.ds(0, 8), :] = jnp.zeros((8, cb), F32)
    pad_ref[pl.ds(8 + l, 8), :] = jnp.zeros((8, cb), F32)

    def conv(u_ref, w_ref, b_ref):
        pad_ref[pl.ds(8, n_meta), :] = u_ref[pl.ds(n_real, n_meta), :].astype(F32)
        pad_ref[pl.ds(8 + n_meta, n_real), :] = u_ref[pl.ds(0, n_real), :].astype(F32)
        w = w_ref[...]
        b = b_ref[...]

        def at(start, n):
            return (w[0:1] * pad_ref[pl.ds(start - 1, n), :] + w[1:2] * pad_ref[pl.ds(start, n), :]
                    + w[2:3] * pad_ref[pl.ds(start + 1, n), :] + b)

        return at(8 + n_meta, n_real), at(8, n_meta)

    x0_real, x0_meta = conv(x0_ref, w0_ref, b0_ref)
    x0c_ref[pl.ds(0, n_real), :] = x0_real
    x0c_ref[pl.ds(n_real, n_meta), :] = x0_meta
    x1_real, x1_meta = conv(x1_ref, w1_ref, b1_ref)
    v_real, v_meta = conv(v_ref, wv_ref, bv_ref)
    vx_ref[pl.ds(0, n_real), :] = v_real * x1_real
    vx_ref[pl.ds(n_real, n_meta), :] = v_meta * x1_meta
    lh = l // 2
    lhp = vxe_ref.shape[0]
    for par, o_ref in ((0, vxe_ref), (1, vxo_ref)):
        o_ref[pl.ds(0, lh), :] = vx_ref[pl.ds(par, lh, stride=2), :].astype(BF16)
        o_ref[pl.ds(lh, lhp - lh), :] = jnp.zeros((lhp - lh, cb), BF16)


def _hyena_prep(proj3, conv_w, conv_b, width, n_real, n_meta, lhp):
    b, l, _ = proj3.shape
    cb = 128
    nc = width // cb
    col = lambda off: pl.BlockSpec((None, l, cb), lambda bi, c: (bi, 0, off + c))
    wcol = lambda off: pl.BlockSpec((3, cb), lambda bi, c: (0, off + c))
    bcol = lambda off: pl.BlockSpec((1, cb), lambda bi, c: (0, off + c))
    cbias = conv_b.reshape(1, 3 * width)
    half = pl.BlockSpec((None, lhp, cb), lambda bi, c: (bi, 0, c))
    return pl.pallas_call(
        functools.partial(_hyena_prep_kernel, n_real=n_real, n_meta=n_meta),
        grid=(b, nc),
        in_specs=[col(0), col(nc), col(2 * nc), wcol(0), wcol(nc), wcol(2 * nc),
                  bcol(0), bcol(nc), bcol(2 * nc)],
        out_specs=[pl.BlockSpec((None, l, cb), lambda bi, c: (bi, 0, c)), half, half],
        out_shape=[jax.ShapeDtypeStruct((b, l, width), F32)]
        + [jax.ShapeDtypeStruct((b, lhp, width), BF16)] * 2,
        scratch_shapes=[pltpu.VMEM((l + 16, cb), F32), pltpu.VMEM((l, cb), F32)],
        compiler_params=_params(("arbitrary", "arbitrary")),
    )(proj3, proj3, proj3, conv_w, conv_w, conv_w, cbias, cbias, cbias)


def _hyena_fwd_kernel(ce_ref, se_ref, co_ref, so_ref, vxe_ref, vxo_ref, k1re_ref, k1im_ref, k2re_ref, k2im_ref,
                      ere_ref, eim_ref, ore_ref, oim_ref):
    vxe = vxe_ref[...]
    vxo = vxo_ref[...]
    a_re = jnp.dot(ce_ref[...], vxe, preferred_element_type=F32)
    b_re = jnp.dot(co_ref[...], vxo, preferred_element_type=F32)
    a_im = jnp.dot(se_ref[...], vxe, preferred_element_type=F32)
    b_im = jnp.dot(so_ref[...], vxo, preferred_element_type=F32)
    u1re, u1im = a_re + b_re, a_im + b_im
    u2re, u2im = a_re - b_re, b_im - a_im
    k1re, k1im, k2re, k2im = k1re_ref[...], k1im_ref[...], k2re_ref[...], k2im_ref[...]
    p_re = u1re * k1re + u1im * k1im
    p_im = u1re * k1im - u1im * k1re
    q_re = u2re * k2re + u2im * k2im
    q_im = u2re * k2im - u2im * k2re
    ere_ref[...] = (p_re + q_re).astype(BF16)
    eim_ref[...] = (p_im - q_im).astype(BF16)
    ore_ref[...] = (p_re - q_re).astype(BF16)
    oim_ref[...] = (p_im + q_im).astype(BF16)


def _hyena_fwd(tabs_e, tabs_o, vxe, vxo, kspec):
    (ce, se), (co, so) = tabs_e, tabs_o
    f2p, lhp = ce.shape
    b, _, width = vxe.shape
    tf = _half_tile(f2p)
    cb = 512
    tab = pl.BlockSpec((tf, lhp), lambda c, bi, f: (f, 0))
    vx_blk = pl.BlockSpec((None, lhp, cb), lambda c, bi, f: (bi, 0, c))
    k_blk = pl.BlockSpec((tf, cb), lambda c, bi, f: (f, c))
    return pl.pallas_call(
        _hyena_fwd_kernel,
        grid=(width // cb, b, f2p // tf),
        in_specs=[tab, tab, tab, tab, vx_blk, vx_blk, k_blk, k_blk, k_blk, k_blk],
        out_specs=[pl.BlockSpec((None, tf, cb), lambda c, bi, f: (bi, f, c))] * 4,
        out_shape=[jax.ShapeDtypeStruct((b, f2p, width), BF16)] * 4,
        compiler_params=_params(("arbitrary", "arbitrary", "arbitrary")),
    )(ce, se, co, so, vxe, vxo, *kspec)


def _hyena_inv_kernel(cet_ref, set_ref, cot_ref, sot_ref, ere_ref, eim_ref, ore_ref, oim_ref,
                      vxe_ref, vxo_ref, x0c_ref, skip_ref, o_ref, mix_ref):
    tr = vxe_ref.shape[0]
    skip = skip_ref[...]
    ye = (jnp.dot(cet_ref[...], ere_ref[...], preferred_element_type=F32)
          - jnp.dot(set_ref[...], eim_ref[...], preferred_element_type=F32))
    yo = (jnp.dot(cot_ref[...], ore_ref[...], preferred_element_type=F32)
          - jnp.dot(sot_ref[...], oim_ref[...], preferred_element_type=F32))
    te = ye + skip * vxe_ref[...].astype(F32)
    to = yo + skip * vxo_ref[...].astype(F32)
    for c in range(o_ref.shape[1] // V7X_LANES):
        lanes = slice(c * V7X_LANES, (c + 1) * V7X_LANES)
        mix_ref[c, pl.ds(0, tr, stride=2), :] = te[:, lanes]
        mix_ref[c, pl.ds(1, tr, stride=2), :] = to[:, lanes]
        o_ref[:, lanes] = mix_ref[c] * x0c_ref[:, lanes]


def _hyena_inv(tabs_et, tabs_ot, spec, vxe, vxo, x0c, skip):
    (cet, set_), (cot, sot) = tabs_et, tabs_ot
    lhp, f2p = cet.shape
    b, l, width = x0c.shape
    tr = _half_tile(lhp)
    cb = 512
    tab = pl.BlockSpec((tr, f2p), lambda c, bi, r: (r, 0))
    s_blk = pl.BlockSpec((None, f2p, cb), lambda c, bi, r: (bi, 0, c))
    vx_blk = pl.BlockSpec((None, tr, cb), lambda c, bi, r: (bi, r, c))
    row_blk = pl.BlockSpec((None, 2 * tr, cb), lambda c, bi, r: (bi, r, c))
    return pl.pallas_call(
        _hyena_inv_kernel,
        grid=(width // cb, b, lhp // tr),
        in_specs=[tab, tab, tab, tab, s_blk, s_blk, s_blk, s_blk, vx_blk, vx_blk, row_blk,
                  pl.BlockSpec((1, cb), lambda c, bi, r: (0, c))],
        out_specs=row_blk,
        out_shape=jax.ShapeDtypeStruct((b, l, width), F32),
        scratch_shapes=[pltpu.VMEM((cb // V7X_LANES, 2 * tr, V7X_LANES), F32)],
        compiler_params=_params(("arbitrary", "arbitrary", "arbitrary")),
    )(cet, set_, cot, sot, *spec, vxe, vxo, x0c, skip.reshape(1, width))


def _fnet_chan_kernel(u_ref, cc_ref, sc_ref, ae_ref, be_ref, ao_ref, bo_ref, *, l, tr):
    row = pl.program_id(1) * tr + lax.broadcasted_iota(jnp.int32, (tr, 1), 0)
    u = jnp.where(row < l, u_ref[...], 0.0).astype(BF16)
    cc = cc_ref[...].astype(BF16)
    sc = sc_ref[...].astype(BF16)
    pick = lax.broadcasted_iota(jnp.int32, (tr // 2, tr), 1) - 2 * lax.broadcasted_iota(jnp.int32, (tr // 2, tr), 0)
    gw = FNET_GROUP_W
    for par, a_ref, b_ref in ((0, ae_ref, be_ref), (1, ao_ref, bo_ref)):
        sel = jnp.where(pick == par, 1.0, 0.0).astype(BF16)
        up = jnp.dot(sel, u, preferred_element_type=F32).astype(BF16)
        for g in range(u.shape[1] // gw):
            ug = up[:, g * gw:(g + 1) * gw]
            a_ref[:, g * gw:(g + 1) * gw] = jnp.dot(ug, cc, preferred_element_type=F32).astype(BF16)
            b_ref[:, g * gw:(g + 1) * gw] = jnp.dot(ug, sc, preferred_element_type=F32).astype(BF16)


def _fnet_chan(proj3, col_blk, width, lhp):
    b, l, _ = proj3.shape
    tr = 256
    k = np.arange(FNET_GROUP_W)
    ang = 2.0 * np.pi * ((k[:, None] * k[None, :]) % FNET_GROUP_W) / FNET_GROUP_W
    cc = jnp.asarray(np.cos(ang), F32)
    sc = jnp.asarray(np.sin(ang), F32)
    return pl.pallas_call(
        functools.partial(_fnet_chan_kernel, l=l, tr=tr),
        grid=(b, 2 * lhp // tr),
        in_specs=[pl.BlockSpec((None, tr, width), lambda bi, r: (bi, r, col_blk)),
                  pl.BlockSpec((FNET_GROUP_W, FNET_GROUP_W), lambda bi, r: (0, 0)),
                  pl.BlockSpec((FNET_GROUP_W, FNET_GROUP_W), lambda bi, r: (0, 0))],
        out_specs=[pl.BlockSpec((None, tr // 2, width), lambda bi, r: (bi, r, 0))] * 4,
        out_shape=[jax.ShapeDtypeStruct((b, lhp, width), BF16)] * 4,
        compiler_params=_params(("arbitrary", "arbitrary")),
    )(proj3, cc, sc)


def _fnet_time_kernel(ce_ref, se_ref, co_ref, so_ref, ae_ref, be_ref, ao_ref, bo_ref, o_ref, *, scale):
    p = (jnp.dot(ce_ref[...], ae_ref[...], preferred_element_type=F32)
         - jnp.dot(se_ref[...], be_ref[...], preferred_element_type=F32))
    q = (jnp.dot(co_ref[...], ao_ref[...], preferred_element_type=F32)
         - jnp.dot(so_ref[...], bo_ref[...], preferred_element_type=F32))
    o_ref[0] = (p + q) * scale
    o_ref[1] = (p - q) * scale


def _fnet_time(tabs_e, tabs_o, parts, l):
    (ce, se), (co, so) = tabs_e, tabs_o
    lhp = ce.shape[0]
    b, _, width = parts[0].shape
    tr = _half_tile(lhp)
    cb = 512
    scale = 1.0 / math.sqrt(l * FNET_GROUP_W)
    tab = pl.BlockSpec((tr, lhp), lambda c, bi, r: (r, 0))
    blk = pl.BlockSpec((None, lhp, cb), lambda c, bi, r: (bi, 0, c))
    return pl.pallas_call(
        functools.partial(_fnet_time_kernel, scale=scale),
        grid=(width // cb, b, lhp // tr),
        in_specs=[tab, tab, tab, tab, blk, blk, blk, blk],
        out_specs=pl.BlockSpec((None, 2, tr, cb), lambda c, bi, r: (bi, 0, r, c)),
        out_shape=jax.ShapeDtypeStruct((b, 2, l // 2, width), F32),
        compiler_params=_params(("arbitrary", "arbitrary", "arbitrary")),
    )(ce, se, co, so, *parts)


def _conformer_kernel(a_ref, g_ref, w_ref, b_ref, o_ref, pad_ref, *, n_real, n_meta, taps, chunk):
    cb = pad_ref.shape[1]
    half = taps // 2
    lead = _round_up(half, 8)
    l = n_real + n_meta
    pad_ref[pl.ds(0, lead), :] = jnp.zeros((lead, cb), F32)
    pad_ref[pl.ds(lead + l, lead), :] = jnp.zeros((lead, cb), F32)
    def glu(start, n):
        return (a_ref[pl.ds(start, n), :].astype(F32)
                * jax.nn.sigmoid(g_ref[pl.ds(start, n), :].astype(F32)))

    pad_ref[pl.ds(lead, n_meta), :] = glu(n_real, n_meta)
    pad_ref[pl.ds(lead + n_meta, n_real), :] = glu(0, n_real)
    w = w_ref[...]
    bias = b_ref[...]

    def window(start, n):
        acc = jnp.broadcast_to(bias, (n, cb))
        for j in range(taps):
            acc = acc + w[j:j + 1] * pad_ref[pl.ds(start + j - half, n), :]
        return acc

    def body(ci, carry):
        base = pl.multiple_of(ci * chunk, chunk)
        o_ref[pl.ds(base, chunk), :] = window(base + lead + n_meta, chunk)
        return carry

    lax.fori_loop(0, n_real // chunk, body, 0)
    o_ref[pl.ds(n_real, n_meta), :] = window(lead, n_meta)


def _conformer_conv(proj3, col_blk, dw_w, dw_b, width, n_real, n_meta):
    b, l, _ = proj3.shape
    taps = dw_w.shape[0]
    cb = 128
    nc = width // cb
    lead = _round_up(taps // 2, 8)
    return pl.pallas_call(
        functools.partial(_conformer_kernel, n_real=n_real, n_meta=n_meta, taps=taps, chunk=64),
        grid=(b, nc),
        in_specs=[pl.BlockSpec((None, l, cb), lambda bi, c: (bi, 0, col_blk + c)),
                  pl.BlockSpec((None, l, cb), lambda bi, c: (bi, 0, col_blk + nc + c)),
                  pl.BlockSpec((taps, cb), lambda bi, c: (0, c)),
                  pl.BlockSpec((1, cb), lambda bi, c: (0, c))],
        out_specs=pl.BlockSpec((None, l, cb), lambda bi, c: (bi, 0, c)),
        out_shape=jax.ShapeDtypeStruct((b, l, width), F32),
        scratch_shapes=[pltpu.VMEM((l + 2 * lead, cb), F32)],
        compiler_params=_params(("arbitrary", "arbitrary")),
    )(proj3, proj3, dw_w, dw_b.reshape(1, width))


def _softmax_pv(parts, sink, out_dtype=F32):
    def rowwise(reduce_fn, combine, arrays):
        groups = {}
        for a in arrays:
            groups.setdefault(a.shape, []).append(a)
        return [reduce_fn(functools.reduce(combine, g), axis=-1, keepdims=True) for g in groups.values()]

    m = functools.reduce(jnp.maximum, rowwise(jnp.max, jnp.maximum, [s for s, _ in parts]), sink)
    es = [jnp.exp(s - m) for s, _ in parts]
    denom = functools.reduce(jnp.add, rowwise(jnp.sum, jnp.add, es), jnp.exp(sink - m))
    acc = None
    for e, (_, v) in zip(es, parts):
        pv = jnp.dot(e.astype(BF16), v, preferred_element_type=F32)
        acc = pv if acc is None else acc + pv
    return (acc / denom).astype(out_dtype)


def _qk(q, k):
    return lax.dot_general(q, k, (((1,), (1,)), ((), ())), preferred_element_type=F32)


def _attn_kernel(slope_ref, sink_ref, q_ref, kp_ref, kc_ref, kn_ref, km_ref, vp_ref, vc_ref, vn_ref, vm_ref,
                 o_ref, *, nb, n_meta, q_per_kv):
    g = pl.program_id(1)
    i = pl.program_id(2)
    t = ATT_BLOCK
    rows = q_per_kv * t
    row = lax.broadcasted_iota(jnp.int32, (rows, t), 0)
    a = row & (t - 1)
    c = lax.broadcasted_iota(jnp.int32, (rows, t), 1)
    hrow = lax.broadcasted_iota(jnp.int32, (rows, 1), 0) // t
    slope = jnp.zeros((rows, 1), F32)
    sink = jnp.zeros((rows, 1), F32)
    for hh in range(q_per_kv):
        slope = jnp.where(hrow == hh, slope_ref[g * q_per_kv + hh], slope)
        sink = jnp.where(hrow == hh, sink_ref[g * q_per_kv + hh], sink)
    scale = 1.0 / math.sqrt(HEAD_DIM)
    q = jnp.concatenate([q_ref[:, hh * HEAD_DIM:(hh + 1) * HEAD_DIM] for hh in range(q_per_kv)], axis=0)
    q = (q.astype(F32) * scale).astype(BF16)
    km = km_ref[...].astype(BF16)
    vm = vm_ref[...].astype(BF16)
    kn = kn_ref[...].astype(BF16)
    vn = vn_ref[...].astype(BF16)

    def store(o):
        o_ref[...] = jnp.concatenate([o[hh * t:(hh + 1) * t] for hh in range(q_per_kv)], axis=1)

    @pl.when(i < nb)
    def _():
        d_prev = (t + a - c).astype(F32)
        d_cur = jnp.abs(a - c).astype(F32)
        d_next = (t + c - a).astype(F32)
        ok_prev = c >= a + jnp.where(i > 0, 0, t)
        ok_next = c <= a - jnp.where(i < nb - 1, 0, t)
        kp, kc = kp_ref[...].astype(BF16), kc_ref[...].astype(BF16)
        vp, vc = vp_ref[...].astype(BF16), vc_ref[...].astype(BF16)
        s_m = _qk(q, km)
        s_p = jnp.where(ok_prev, _qk(q, kp) - slope * d_prev, NEG)
        s_c = _qk(q, kc) - slope * d_cur
        s_n = jnp.where(ok_next, _qk(q, kn) - slope * d_next, NEG)
        store(_softmax_pv([(s_m, vm), (s_p, vp), (s_c, vc), (s_n, vn)], sink))

    @pl.when(i == nb)
    def _():
        delta = n_meta + c - a
        s_m = _qk(q, km)
        s_0 = jnp.where(delta <= t, _qk(q, kn) - slope * delta.astype(F32), NEG)
        store(_softmax_pv([(s_m, vm), (s_0, vn)], sink))


def _windowed_attention(proj3, sink, q_col, k_col, v_col, n_q_heads, n_real, n_meta):
    b, l, _ = proj3.shape
    nb = n_real // ATT_BLOCK
    q_per_kv = n_q_heads // N_KV_HEADS
    qw = q_per_kv * HEAD_DIM
    qb, kb, vb = q_col // qw, k_col // HEAD_DIM, v_col // HEAD_DIM
    mb = n_real // n_meta
    slopes = jnp.asarray([2.0 ** (-8.0 * (h + 1) / n_q_heads) for h in range(n_q_heads)], F32)
    smem = pl.BlockSpec(memory_space=pltpu.SMEM)

    def kv_blk(colb, shift):
        def index(bi, g, i):
            blk = jnp.clip(i + shift, 0, nb - 1)
            if shift == 1:
                blk = jnp.where(i == nb, 0, blk)
            return bi, blk, colb + g
        return pl.BlockSpec((None, ATT_BLOCK, HEAD_DIM), index)

    kv_meta = lambda colb: pl.BlockSpec((None, n_meta, HEAD_DIM), lambda bi, g, i: (bi, mb, colb + g))
    return pl.pallas_call(
        functools.partial(_attn_kernel, nb=nb, n_meta=n_meta, q_per_kv=q_per_kv),
        grid=(b, N_KV_HEADS, nb + 1),
        in_specs=[smem, smem,
                  pl.BlockSpec((None, ATT_BLOCK, qw), lambda bi, g, i: (bi, i, qb + g)),
                  kv_blk(kb, -1), kv_blk(kb, 0), kv_blk(kb, 1), kv_meta(kb),
                  kv_blk(vb, -1), kv_blk(vb, 0), kv_blk(vb, 1), kv_meta(vb)],
        out_specs=pl.BlockSpec((None, ATT_BLOCK, qw), lambda bi, g, i: (bi, i, g)),
        out_shape=jax.ShapeDtypeStruct((b, l, n_q_heads * HEAD_DIM), F32),
        compiler_params=_params(("arbitrary", "arbitrary", "arbitrary")),
    )(slopes, sink, proj3, proj3, proj3, proj3, proj3, proj3, proj3, proj3, proj3)


def _group_norm_kernel(ya_ref, yb_ref, yc_ref, yd_ref, g_ref, lng_ref, lnb_ref, o_ref, *, widths):
    def rms(y, g):
        return y * lax.rsqrt(jnp.mean(y * y, axis=-1, keepdims=True) + EPS) * g

    off = 0
    for idx, (y_ref, w) in enumerate(zip((ya_ref, yb_ref, yc_ref, yd_ref), widths)):
        y = y_ref[...]
        if idx == 2:
            mu = jnp.mean(y, axis=-1, keepdims=True)
            yc = y - mu
            var = jnp.mean(yc * yc, axis=-1, keepdims=True)
            y = yc * lax.rsqrt(var + EPS) * lng_ref[...] + lnb_ref[...]
            y = y * jax.nn.sigmoid(y)
        o_ref[:, off:off + w] = rms(y, g_ref[:, off:off + w]).astype(o_ref.dtype)
        off += w


def _group_norms(ya, yb, yc, yd, group_g, ln_g, ln_b, tm=256):
    m = ya.shape[0]
    widths = (ya.shape[1], yb.shape[1], yc.shape[1], yd.shape[1])
    tot = sum(widths)
    row = lambda w: pl.BlockSpec((tm, w), lambda i: (i, 0))
    full = lambda w: pl.BlockSpec((1, w), lambda i: (0, 0))
    return pl.pallas_call(
        functools.partial(_group_norm_kernel, widths=widths),
        grid=(pl.cdiv(m, tm),),
        in_specs=[row(widths[0]), row(widths[1]), row(widths[2]), row(widths[3]),
                  full(tot), full(widths[2]), full(widths[2])],
        out_specs=row(tot),
        out_shape=jax.ShapeDtypeStruct((m, tot), BF16),
        compiler_params=_params(("arbitrary",)),
    )(ya, yb, yc, yd, group_g.reshape(1, tot), ln_g.reshape(1, -1), ln_b.reshape(1, -1))


def _norm_router_kernel(h_ref, g_ref, w_ref, o_ref):
    x = h_ref[...]
    ms = jnp.mean(x * x, axis=-1, keepdims=True)
    xn = (x * lax.rsqrt(ms + EPS) * g_ref[...]).astype(BF16)
    o_ref[...] = lax.dot_general(w_ref[...], xn, (((1,), (1,)), ((), ())), preferred_element_type=F32)


def _norm_router(h3, g, router_w, n_real, tm=256):
    b, _, d = h3.shape
    nt = n_real // tm
    wt = router_w.T.astype(BF16)
    return pl.pallas_call(
        _norm_router_kernel,
        grid=(b, nt),
        in_specs=[pl.BlockSpec((None, tm, d), lambda bi, i: (bi, i, 0)),
                  pl.BlockSpec((1, d), lambda bi, i: (0, 0)),
                  pl.BlockSpec((N_EXPERTS, d), lambda bi, i: (0, 0))],
        out_specs=pl.BlockSpec((N_EXPERTS, tm), lambda bi, i: (0, bi * nt + i)),
        out_shape=jax.ShapeDtypeStruct((N_EXPERTS, b * n_real), F32),
        compiler_params=_params(("arbitrary", "arbitrary")),
    )(h3, g.reshape(1, d), wt)


def _route_kernel(lg_ref, b_ref, slot1_ref, slot2_ref, g1_ref, g2_ref, te_ref, tv_ref, nx_ref, fp_ref, *,
                  chunk, n_slots):
    ne, m = lg_ref.shape
    logits = lg_ref[...] + b_ref[...]
    row = lax.broadcasted_iota(jnp.int32, (ne, m), 0)
    m1 = jnp.max(logits, axis=0, keepdims=True)
    e1 = jnp.min(jnp.where(logits == m1, row, ne), axis=0, keepdims=True)
    rest = jnp.where(row == e1, -jnp.inf, logits)
    m2 = jnp.max(rest, axis=0, keepdims=True)
    e2 = jnp.min(jnp.where(rest == m2, row, ne), axis=0, keepdims=True)
    g1 = 1.0 / (1.0 + jnp.exp(m2 - m1))
    g1_ref[...] = g1
    g2_ref[...] = 1.0 - g1
    oh1 = (row == e1).astype(F32)
    oh2 = (row == e2).astype(F32)
    c1 = jnp.sum(oh1, axis=1, keepdims=True)
    c2 = jnp.sum(oh2, axis=1, keepdims=True)
    padded = jnp.floor((c1 + c2 + (MOE_TILE - 1)) / MOE_TILE) * MOE_TILE
    erow = lax.broadcasted_iota(jnp.int32, (ne, 1), 0)
    start = jnp.zeros((ne, 1), F32)
    for e in range(ne - 1):
        start = start + jnp.where(erow > e, padded[e:e + 1, :], 0.0)
    end = start + padded
    tri = (lax.broadcasted_iota(jnp.int32, (chunk, chunk), 0)
           < lax.broadcasted_iota(jnp.int32, (chunk, chunk), 1)).astype(BF16)
    carry = jnp.concatenate([jnp.zeros((ne, 1), F32), c1], axis=0)
    base1 = start
    for ci in range(m // chunk):
        sl = slice(ci * chunk, (ci + 1) * chunk)
        oh = jnp.concatenate([oh1[:, sl], oh2[:, sl]], axis=0)
        pre = jnp.dot(oh.astype(BF16), tri, preferred_element_type=F32) + carry
        carry = carry + jnp.sum(oh, axis=1, keepdims=True)
        s1 = jnp.sum(oh[:ne] * (pre[:ne] + base1), axis=0, keepdims=True)
        s2 = jnp.sum(oh[ne:] * (pre[ne:] + base1), axis=0, keepdims=True)
        slot1_ref[:, sl] = s1.astype(jnp.int32)
        slot2_ref[:, sl] = s2.astype(jnp.int32)
    tile0 = (lax.broadcasted_iota(jnp.int32, (1, V7X_LANES), 1) * MOE_TILE).astype(F32)
    te = jnp.sum((end <= tile0).astype(jnp.int32), axis=0, keepdims=True)
    nonempty = padded > 0.0
    last = jnp.max(jnp.where(nonempty, erow, 0), axis=0, keepdims=True)
    te = jnp.minimum(te, last)
    te_ref[...] = te
    tv_ref[...] = (tile0 < end[ne - 1:ne, :]).astype(jnp.int32)
    nx_ref[...] = jnp.min(jnp.where(nonempty & (erow > te), erow, ne), axis=0, keepdims=True)
    lane = lax.broadcasted_iota(jnp.int32, (ne, V7X_LANES), 1)
    first_pad = jnp.floor((start + c1 + c2) / PAD_FILL_ALIGN) * PAD_FILL_ALIGN
    first_pad = jnp.minimum(first_pad, float(n_slots - PAD_FILL_ROWS)).astype(jnp.int32)
    fp_ref[...] = jnp.sum(jnp.where(lane == erow, first_pad, 0), axis=0, keepdims=True)


def _route(logits_t, router_b, n_slots):
    ne, m = logits_t.shape
    assert n_slots // MOE_TILE <= V7X_LANES
    vec = lambda dt: jax.ShapeDtypeStruct((1, m), dt)
    lane = jax.ShapeDtypeStruct((1, V7X_LANES), jnp.int32)
    return pl.pallas_call(
        functools.partial(_route_kernel, chunk=_pick_tile(m, (512, 256, 128)), n_slots=n_slots),
        out_shape=[vec(jnp.int32), vec(jnp.int32), vec(F32), vec(F32), lane, lane, lane, lane],
        compiler_params=_params(None),
    )(logits_t, router_b.reshape(ne, 1))


def _row_copy(src_hbm, dst_vmem, sem, src_row, dst_row):
    return pltpu.make_async_copy(src_hbm.at[pl.ds(src_row, 1)], dst_vmem.at[pl.ds(dst_row, 1)], sem)


def _norm_scatter_kernel(s1_ref, s2_ref, fp_ref, tv_ref, h_ref, g_ref, xs_hbm, pk_ref, zero_ref, sem, zsem, *,
                         n_steps, n_tiles):
    t = pl.program_id(0) * pl.num_programs(1) + pl.program_id(1)

    @pl.when(t == 0)
    def _():
        zero_ref[...] = jnp.zeros_like(zero_ref)

        def fill(row, n):
            return pltpu.make_async_copy(zero_ref.at[pl.ds(0, n)], xs_hbm.at[pl.ds(row, n)], zsem)

        def fill_now(row, n):
            fill(row, n).start()
            fill(row, n).wait()

        for e in range(N_EXPERTS):
            fill_now(pl.multiple_of(fp_ref[e], PAD_FILL_ALIGN), PAD_FILL_ROWS)
        for tile in range(n_tiles):
            @pl.when(tv_ref[tile] == 0)
            def _():
                fill_now(tile * MOE_TILE, MOE_TILE)

    cur = t % 2
    tm = h_ref.shape[0]
    x = h_ref[...]
    ms = jnp.mean(x * x, axis=-1, keepdims=True)
    pk_ref[cur] = x * lax.rsqrt(ms + EPS) * g_ref[...]

    def row_out(buf, r, slot):
        return pltpu.make_async_copy(pk_ref.at[buf, pl.ds(r, 1)], xs_hbm.at[pl.ds(slot, 1)], sem.at[buf])

    def start(r4, carry):
        for k in range(ROW_DMA_UNROLL):
            r = r4 * ROW_DMA_UNROLL + k
            row_out(cur, r, s1_ref[0, 0, r]).start()
            row_out(cur, r, s2_ref[0, 0, r]).start()
        return carry

    lax.fori_loop(0, tm // ROW_DMA_UNROLL, start, 0)

    def wait_all(buf):
        def wait(r, carry):
            row_out(buf, r, 0).wait()
            row_out(buf, r, 0).wait()
            return carry

        lax.fori_loop(0, tm, wait, 0)

    @pl.when(t > 0)
    def _():
        wait_all(1 - cur)

    @pl.when(t == n_steps - 1)
    def _():
        wait_all(cur)


def _norm_scatter(h3, g, slot1, slot2, first_pad, tile_valid, n_real, n_slots, tm=256):
    b, _, d = h3.shape
    nt = n_real // tm
    idx = lambda a: a.reshape(b * nt, 1, tm)
    smem_blk = pl.BlockSpec((1, 1, tm), lambda bi, i: (bi * nt + i, 0, 0), memory_space=pltpu.SMEM)
    smem = pl.BlockSpec(memory_space=pltpu.SMEM)
    return pl.pallas_call(
        functools.partial(_norm_scatter_kernel, n_steps=b * nt, n_tiles=n_slots // MOE_TILE),
        grid=(b, nt),
        in_specs=[smem_blk, smem_blk, smem, smem,
                  pl.BlockSpec((None, tm, d), lambda bi, i: (bi, i, 0)),
                  pl.BlockSpec((1, d), lambda bi, i: (0, 0))],
        out_specs=pl.BlockSpec(memory_space=pl.ANY),
        out_shape=jax.ShapeDtypeStruct((n_slots, d), F32),
        scratch_shapes=[pltpu.VMEM((2, tm, d), F32), pltpu.VMEM((PAD_FILL_ROWS, d), F32),
                        pltpu.SemaphoreType.DMA((2,)), pltpu.SemaphoreType.DMA(())],
        compiler_params=_params(("arbitrary", "arbitrary")),
    )(idx(slot1), idx(slot2), first_pad, tile_valid, h3, g.reshape(1, d))


def _combine_kernel(s1_ref, s2_ref, n1_ref, n2_ref, h_ref, g1_ref, g2_ref, fg_ref, y_hbm, o_ref,
                    a_ref, b_ref, sem, *, n_steps):
    t = pl.program_id(0) * pl.num_programs(1) + pl.program_id(1)
    cur = t % 2
    n = a_ref.shape[1]

    def gather(buf, i1_ref, i2_ref):
        def start(r4, carry):
            for k in range(ROW_DMA_UNROLL):
                r = r4 * ROW_DMA_UNROLL + k
                _row_copy(y_hbm, a_ref.at[buf], sem.at[buf], i1_ref[0, 0, r], r).start()
                _row_copy(y_hbm, b_ref.at[buf], sem.at[buf], i2_ref[0, 0, r], r).start()
            return carry

        lax.fori_loop(0, n // ROW_DMA_UNROLL, start, 0)

    @pl.when(t == 0)
    def _():
        gather(0, s1_ref, s2_ref)

    @pl.when(t + 1 < n_steps)
    def _():
        gather(1 - cur, n1_ref, n2_ref)

    def wait(r, carry):
        _row_copy(y_hbm, a_ref.at[cur], sem.at[cur], 0, r).wait()
        _row_copy(y_hbm, b_ref.at[cur], sem.at[cur], 0, r).wait()
        return carry

    lax.fori_loop(0, n, wait, 0)
    h = h_ref[...] + g1_ref[...] * a_ref[cur] + g2_ref[...] * b_ref[cur]
    ms = jnp.mean(h * h, axis=-1, keepdims=True)
    o_ref[...] = h * lax.rsqrt(ms + EPS) * fg_ref[...]


def _moe_combine_final(h3, y, slot1, slot2, g1, g2, final_g, n_real, tm=256):
    b, _, d = h3.shape
    nt = n_real // tm
    idx = lambda a: a.reshape(b * nt, 1, tm)
    gate = lambda a: a.reshape(b, n_real, 1)
    smem_blk = pl.BlockSpec((1, 1, tm), lambda bi, i: (bi * nt + i, 0, 0), memory_space=pltpu.SMEM)
    smem_next = pl.BlockSpec((1, 1, tm), lambda bi, i: (jnp.minimum(bi * nt + i + 1, b * nt - 1), 0, 0),
                             memory_space=pltpu.SMEM)
    return pl.pallas_call(
        functools.partial(_combine_kernel, n_steps=b * nt),
        grid=(b, nt),
        in_specs=[smem_blk, smem_blk, smem_next, smem_next,
                  pl.BlockSpec((None, tm, d), lambda bi, i: (bi, i, 0)),
                  pl.BlockSpec((None, tm, 1), lambda bi, i: (bi, i, 0)),
                  pl.BlockSpec((None, tm, 1), lambda bi, i: (bi, i, 0)),
                  pl.BlockSpec((1, d), lambda bi, i: (0, 0)),
                  pl.BlockSpec(memory_space=pl.ANY)],
        out_specs=pl.BlockSpec((None, tm, d), lambda bi, i: (bi, i, 0)),
        out_shape=jax.ShapeDtypeStruct((b, n_real, d), F32),
        scratch_shapes=[pltpu.VMEM((2, tm, d), F32), pltpu.VMEM((2, tm, d), F32),
                        pltpu.SemaphoreType.DMA((2,))],
        compiler_params=_params(("arbitrary", "arbitrary")),
    )(idx(slot1), idx(slot2), idx(slot1), idx(slot2), h3, gate(g1), gate(g2), final_g.reshape(1, d), y)


def _filter_features(l, lhp, n_meta, n_emb):
    j = np.arange(lhp)
    r = np.concatenate([2 * j, 2 * j + 1])
    lp = 2 * lhp
    pos = (r + n_meta) % l
    valid = r < l
    t = pos / (l - 1.0)
    bands = (n_emb - 1) // 2
    w = 2.0 * np.pi * pos / l
    f = np.linspace(1e-4, bands - 1, bands)
    z = np.zeros((lp, V7X_LANES), np.float64)
    z[:, 0] = t
    z[:, 1:1 + bands] = np.cos(f[None, :] * w[:, None])
    z[:, 1 + bands:1 + 2 * bands] = -np.sin(f[None, :] * w[:, None])
    z *= valid[:, None]
    z[:, V7X_LANES - 1] = valid
    return jnp.asarray(z, F32)


def kernel(x, meta_tokens, norm_mix_g, w_in, hy_conv_w, hy_conv_b, hy_f_w1, hy_f_b1, hy_f_w2, hy_f_b2, hy_f_w3, hy_f_b3, hy_f_wo, hy_f_freq, hy_decay, hy_skip, cv_dw_w, cv_dw_b, cv_ln_g, cv_ln_b, attn_sink, group_norm_g, w_out, norm_ffn_g, ffn_w1, ffn_w3, ffn_w2, router_w, router_b, moe_w1, moe_w3, moe_w2, final_norm_g):
    b, n_real, d = x.shape
    n_meta = meta_tokens.shape[0]
    depth = w_in.shape[0]
    l = n_real + n_meta
    m = b * l
    w_hy = hy_skip.shape[1]
    w_fn = w_hy
    w_cv = cv_dw_b.shape[1]
    kv_w = N_KV_HEADS * HEAD_DIM
    p_in = w_in.shape[2]
    w_at = p_in - 3 * w_hy - w_fn - 2 * w_cv - 2 * kv_w
    n_q_heads = w_at // HEAD_DIM
    d_ff = ffn_w1.shape[2]
    assert depth == 2 and router_w.shape[2] == N_EXPERTS
    assert n_meta == V7X_BF16_ROWS and n_real % 256 == 0
    off_fn = 3 * w_hy
    off_cv = off_fn + w_fn
    off_q = off_cv + 2 * w_cv
    off_k = off_q + w_at
    off_v = off_k + kv_w

    lh = l // 2
    lhp = _round_up(lh + 1, V7X_LANES)
    j = np.arange(lhp)
    freq = np.arange(lhp)
    hy_unit, hy_period = np.pi / l, 2 * l
    tabs_f, tabs_t, tabs_n = [], [], []
    for par in (0, 1):
        col_pos = (2 * j + par + n_meta) % l
        tabs_f.append(_dft_tables(col_pos, j < lh, hy_unit, hy_period, 0, 1, lhp // 16 - 1,
                                  np.arange(lhp - 16, lhp), lhp))
        tabs_t.append(_dft_tables(freq, freq <= lh, hy_unit, hy_period, n_meta + par, 2, n_real // 32,
                                  np.arange(par, n_meta, 2), lhp))
        tabs_n.append(_dft_tables(col_pos, j < lh, 2 * np.pi / l, l, n_meta, 1, lh // 16,
                                  n_meta + 16 * (lh // 16) + np.arange(lh % 16), lhp))
    edge = np.where(freq == 0, 1.0, 2.0) / (2.0 * l)
    w1 = jnp.asarray(np.where(freq <= lh, edge, 0.0).reshape(lhp, 1), F32)
    w2 = jnp.asarray(np.where(freq < lh, edge, 0.0).reshape(lhp, 1), F32)
    zc = _filter_features(l, lhp, n_meta, hy_f_w1.shape[1])

    h = jnp.concatenate([x, jnp.broadcast_to(meta_tokens[None].astype(x.dtype), (b, n_meta, d))], axis=1)
    h = h.reshape(m, d)
    out = None
    for layer in range(depth):
        xn = _rmsnorm(h, norm_mix_g[layer], BF16)
        proj = _matmul(xn, [w_in], layer, tk=d, k_blk=0, n_out=p_in, tn=512, tm=MM_ROWS, out_dtype=BF16)
        proj3 = proj.reshape(b, l, p_in)

        hs, hd = _hyena_filter_sums(zc, hy_f_w1[layer], hy_f_b1[layer], hy_f_w2[layer], hy_f_b2[layer],
                                    hy_f_w3[layer], hy_f_b3[layer], hy_f_wo[layer], hy_f_freq[layer],
                                    hy_decay[layer], w_hy)
        kspec = _hyena_filter_spectrum(tabs_f[0], tabs_f[1], hs, hd, w1, w2)
        x0c, vxe, vxo = _hyena_prep(proj3, hy_conv_w[layer], hy_conv_b[layer], w_hy, n_real, n_meta, lhp)
        spec = _hyena_fwd(tabs_f[0], tabs_f[1], vxe, vxo, kspec)
        y_a = _hyena_inv(tabs_t[0], tabs_t[1], spec, vxe, vxo, x0c, hy_skip[layer])

        y_b = _fnet_time(tabs_n[0], tabs_n[1], _fnet_chan(proj3, off_fn // w_fn, w_fn, lhp), l)

        y_c = _conformer_conv(proj3, off_cv // 128, cv_dw_w[layer], cv_dw_b[layer], w_cv, n_real, n_meta)
        y_d = _windowed_attention(proj3, attn_sink[layer], off_q, off_k, off_v, n_q_heads, n_real, n_meta)

        ycat = _group_norms(y_a.reshape(m, w_hy), y_b.reshape(m, w_fn), y_c.reshape(m, w_cv),
                            y_d.reshape(m, w_at), group_norm_g[layer], cv_ln_g[layer], cv_ln_b[layer])
        h = _matmul(ycat, [w_out], layer, tk=ycat.shape[1], k_blk=0, n_out=d, tn=512, tm=MM_ROWS,
                    out_dtype=F32, residual=h)

        if layer % 2 == 0:
            i = layer // 2
            xn = _rmsnorm(h, norm_ffn_g[layer], BF16)
            tn_ff = _pick_tile(d_ff, (512, 256, 128))
            act = _matmul(xn, [ffn_w1, ffn_w3], i, tk=d, k_blk=0, n_out=d_ff, tn=tn_ff, tm=MM_ROWS,
                          out_dtype=BF16, swiglu=True)
            n_k = 2 if d_ff % (2 * V7X_LANES) == 0 and d_ff > 4096 else 1
            for kb in range(n_k):
                h = _matmul(act, [ffn_w2], i, tk=d_ff // n_k, k_blk=kb, n_out=d, tn=512, tm=1040,
                            out_dtype=F32, residual=h)
        else:
            i = layer // 2
            h3 = h.reshape(b, l, d)
            mr = b * n_real
            logits_t = _norm_router(h3, norm_ffn_g[layer], router_w[i], n_real)
            n_slots = _round_up(2 * mr + N_EXPERTS * (MOE_TILE - 1), MOE_TILE)
            n_tiles = n_slots // MOE_TILE
            slot1, slot2, g1, g2, te, tv, nx, fp = _route(logits_t, router_b[i], n_slots)
            slot1, slot2 = slot1.reshape(mr), slot2.reshape(mr)
            te, tv, nx, fp = (a.reshape(-1)[:n_tiles] for a in (te, tv, nx, fp))
            xs = _norm_scatter(h3, norm_ffn_g[layer], slot1, slot2, fp, tv, n_real, n_slots)
            act = _grouped_matmul(xs, [moe_w1[i], moe_w3[i]], te, tv, nx, tn=512, out_dtype=BF16, swiglu=True)
            ys = _grouped_matmul(act, [moe_w2[i]], te, tv, nx, tn=1024, out_dtype=F32)
            out = _moe_combine_final(h3, ys, slot1, slot2, g1, g2, final_norm_g, n_real)
    return out
```
